```python
import math
import numpy as np
import jax
import jax.numpy as jnp
from jax import lax

D_MODEL = 2048
BATCH = 8
SEQ = 4096
DEPTH = 2
DEC_BATCH = 32
DEC_SEQ = 64
PAST_LEN = 1024

CHUNK = 64
LN_EPS = 1e-5
RMS_EPS = 1e-6
ALPHA = (2 * DEPTH) ** 0.25
BETA = (8 * DEPTH) ** -0.25

LRU_WIDTH = D_MODEL // 2
LRU_BLOCKS = 8
LRU_BLOCK = LRU_WIDTH // LRU_BLOCKS
CONV_WIDTH = 4
LRU_C = 8.0

MLA_HEADS = 16
QK_NOPE = 128
QK_ROPE = 64
V_HEAD = 128
Q_LORA = 512
KV_LORA = 512
ROPE_THETA = 10000.0
Q_BLOCK = 128

S5_WIDTH = D_MODEL // 2
S5_GROUP = 16
S5_GROUPS = S5_WIDTH // S5_GROUP
S5_STATE = 64
DT_MIN = 0.001
DT_MAX = 0.1

N_BRANCH = 3
IN_WIDTHS = (LRU_WIDTH, LRU_WIDTH, Q_LORA, KV_LORA, QK_ROPE, S5_WIDTH, N_BRANCH * D_MODEL)
N_IN = sum(IN_WIDTHS)

N_EXPERTS = 64
TOP_K = 6
N_GROUPS = 8
TOPK_GROUPS = 4
EXPERT_FF = D_MODEL // 4
SHARED_FF = D_MODEL // 4
ROUTE_SCALE = 2.5
EXPERT_BLOCK = 128

kernel_name = 'hybrid_streaming_encoder_step'


def layer_norm(x, g, b):
    xf = x.astype(jnp.float32)
    mu = jnp.mean(xf, axis=-1, keepdims=True)
    var = jnp.mean(jnp.square(xf - mu), axis=-1, keepdims=True)
    return ((xf - mu) * lax.rsqrt(var + LN_EPS) * g + b).astype(x.dtype)


def rms_norm(x, g):
    xf = x.astype(jnp.float32)
    y = xf * lax.rsqrt(jnp.mean(jnp.square(xf), axis=-1, keepdims=True) + RMS_EPS)
    return (y * g).astype(x.dtype)


def rope(x, pos):
    half = x.shape[-1] // 2
    inv = ROPE_THETA ** (-jnp.arange(half, dtype=jnp.float32) / half)
    ang = pos.astype(jnp.float32)[:, None] * inv
    cos = jnp.cos(ang)[:, None, :]
    sin = jnp.sin(ang)[:, None, :]
    x1 = x[..., :half].astype(jnp.float32)
    x2 = x[..., half:].astype(jnp.float32)
    return jnp.concatenate([x1 * cos - x2 * sin, x2 * cos + x1 * sin], axis=-1).astype(x.dtype)


def real_affine_combine(e1, e2):
    a1, b1 = e1
    a2, b2 = e2
    return (a1 * a2, a2 * b1 + b2)


def complex_affine_combine(e1, e2):
    ar1, ai1, br1, bi1 = e1
    ar2, ai2, br2, bi2 = e2
    return (ar1 * ar2 - ai1 * ai2, ar1 * ai2 + ai1 * ar2,
            ar2 * br1 - ai2 * bi1 + br2, ar2 * bi1 + ai2 * br1 + bi2)


def causal_conv(xa, prev, w, b):
    S = xa.shape[1]
    xc = jnp.concatenate([prev.astype(xa.dtype), xa], axis=1)
    y = b
    for k in range(CONV_WIDTH):
        y = y + xc[:, k:k + S] * w[k]
    return y, xc[:, S:]


def rglru_branch(xa, ga, conv_prev, h0, lp):
    B, S, W = xa.shape
    xc, conv_new = causal_conv(xa, conv_prev, lp['lru_conv_w'], lp['lru_conv_b'])
    xb = xc.reshape(B, S, LRU_BLOCKS, LRU_BLOCK)
    r = jax.nn.sigmoid(jnp.einsum('bsnc,ncd->bsnd', xb, lp['lru_gate_a_w']) + lp['lru_gate_a_b']).reshape(B, S, W)
    gi = jax.nn.sigmoid(jnp.einsum('bsnc,ncd->bsnd', xb, lp['lru_gate_x_w']) + lp['lru_gate_x_b']).reshape(B, S, W)
    log_a = -LRU_C * r.astype(jnp.float32) * jax.nn.softplus(-lp['lru_lambda'].astype(jnp.float32))
    a = jnp.exp(log_a)
    b = jnp.sqrt(-jnp.expm1(2.0 * log_a)) * (gi * xc).astype(jnp.float32)
    b = b.at[:, 0].add(a[:, 0] * h0.astype(jnp.float32))
    _, h = lax.associative_scan(real_affine_combine, (a, b), axis=1)
    y = h.astype(xa.dtype) * jax.nn.gelu(ga)
    return y, conv_new, h[:, -1]


def chunk_causal_attention(q_nope, q_rope, k_nope, k_rope, v, q_pos, k_pos):
    scale = (QK_NOPE + QK_ROPE) ** -0.5
    k_chunk = k_pos // CHUNK

    def attend(blk):
        qn, qr, qp = blk
        s = jnp.einsum('bqhd,bkhd->bhqk', qn, k_nope) + jnp.einsum('bqhd,bkd->bhqk', qr, k_rope)
        s = s.astype(jnp.float32) * scale
        mask = k_chunk[None, :] <= (qp // CHUNK)[:, None]
        p = jax.nn.softmax(jnp.where(mask, s, -jnp.inf), axis=-1).astype(v.dtype)
        return jnp.einsum('bhqk,bkhd->bqhd', p, v)

    B, S = q_nope.shape[:2]
    if S <= Q_BLOCK:
        return attend((q_nope, q_rope, q_pos))
    nb = S // Q_BLOCK

    def to_blocks(t):
        return t.reshape((B, nb, Q_BLOCK) + t.shape[2:]).swapaxes(0, 1)

    out = lax.map(attend, (to_blocks(q_nope), to_blocks(q_rope), q_pos.reshape(nb, Q_BLOCK)))
    return out.swapaxes(0, 1).reshape((B, S) + out.shape[3:])


def mla_branch(cq, ckv, kr, ckv_past, kr_past, q_pos, k_pos, lp):
    B, S, _ = cq.shape
    q = jnp.einsum('bsr,rhd->bshd', rms_norm(cq, lp['mla_q_norm_g']), lp['mla_w_uq'])
    q_nope = q[..., :QK_NOPE]
    q_rope = rope(q[..., QK_NOPE:], q_pos)
    ckv_new = rms_norm(ckv, lp['mla_kv_norm_g'])
    kr_new = rope(kr[:, :, None, :], q_pos)[:, :, 0, :]
    ckv_all = jnp.concatenate([ckv_past.astype(ckv_new.dtype), ckv_new], axis=1)
    kr_all = jnp.concatenate([kr_past.astype(kr_new.dtype), kr_new], axis=1)
    k_nope = jnp.einsum('btr,rhd->bthd', ckv_all, lp['mla_w_uk'])
    v = jnp.einsum('btr,rhd->bthd', ckv_all, lp['mla_w_uv'])
    o = chunk_causal_attention(q_nope, q_rope, k_nope, kr_all, v, q_pos, k_pos)
    return o.reshape(B, S, MLA_HEADS * V_HEAD), ckv_new, kr_new


def s5_branch(uc, h0_re, h0_im, lp):
    B, S, _ = uc.shape
    f32 = jnp.float32
    u = uc.astype(f32).reshape(B, S, S5_GROUPS, S5_GROUP)
    dt = jnp.exp(lp['s5_log_dt'].astype(f32))[:, None]
    lam_re = lp['s5_lam_re'].astype(f32)
    lam_im = lp['s5_lam_im'].astype(f32)
    mag = jnp.exp(lam_re * dt)
    a_re = mag * jnp.cos(lam_im * dt)
    a_im = mag * jnp.sin(lam_im * dt)
    den = jnp.square(lam_re) + jnp.square(lam_im)
    f_re = ((a_re - 1.0) * lam_re + a_im * lam_im) / den
    f_im = (a_im * lam_re - (a_re - 1.0) * lam_im) / den
    b_re = lp['s5_b_re'].astype(f32)
    b_im = lp['s5_b_im'].astype(f32)
    bb_re = f_re[..., None] * b_re - f_im[..., None] * b_im
    bb_im = f_re[..., None] * b_im + f_im[..., None] * b_re
    bu_re = jnp.einsum('bsgc,gpc->bsgp', u, bb_re)
    bu_im = jnp.einsum('bsgc,gpc->bsgp', u, bb_im)
    h0_re = h0_re.astype(f32)
    h0_im = h0_im.astype(f32)
    bu_re = bu_re.at[:, 0].add(a_re * h0_re - a_im * h0_im)
    bu_im = bu_im.at[:, 0].add(a_re * h0_im + a_im * h0_re)
    a_re_t = jnp.broadcast_to(a_re, (1, S) + a_re.shape)
    a_im_t = jnp.broadcast_to(a_im, (1, S) + a_im.shape)
    _, _, h_re, h_im = lax.associative_scan(complex_affine_combine, (a_re_t, a_im_t, bu_re, bu_im), axis=1)
    y = (jnp.einsum('bsgp,gcp->bsgc', h_re, lp['s5_c_re'].astype(f32))
         - jnp.einsum('bsgp,gcp->bsgc', h_im, lp['s5_c_im'].astype(f32)))
    y = y.reshape(B, S, S5_WIDTH) + lp['s5_d'].astype(f32) * u.reshape(B, S, S5_WIDTH)
    z = jax.nn.gelu(y).astype(uc.dtype)
    out = z * jax.nn.sigmoid(z @ lp['s5_glu_w'] + lp['s5_glu_b'])
    return out, h_re[:, -1], h_im[:, -1]


def routed_experts(xt, idx, wts, w1, w3, w2):
    T, D = xt.shape
    tk = T * TOP_K
    n_blocks = -(-(tk + N_EXPERTS * (EXPERT_BLOCK - 1)) // EXPERT_BLOCK)
    flat_e = idx.reshape(-1)
    order = jnp.argsort(flat_e)
    se = flat_e[order]
    stok = (order // TOP_K).astype(jnp.int32)
    sw = wts.reshape(-1)[order]
    counts = jnp.zeros((N_EXPERTS,), jnp.int32).at[flat_e].add(1)
    padded = (counts + EXPERT_BLOCK - 1) // EXPERT_BLOCK * EXPERT_BLOCK
    pad_end = jnp.cumsum(padded)
    start = jnp.cumsum(counts) - counts
    dest = (pad_end - padded)[se] + jnp.arange(tk, dtype=jnp.int32) - start[se]
    n_rows = n_blocks * EXPERT_BLOCK
    buf_tok = jnp.full((n_rows,), T, jnp.int32).at[dest].set(stok)
    buf_w = jnp.zeros((n_rows,), wts.dtype).at[dest].set(sw)
    blk_e = jnp.minimum(jnp.searchsorted(pad_end, jnp.arange(n_blocks, dtype=jnp.int32) * EXPERT_BLOCK,
                                         side='right'), N_EXPERTS - 1)
    x_pad = jnp.concatenate([xt, jnp.zeros((1, D), xt.dtype)], axis=0)

    def body(acc, blk):
        tok_b, w_b, e = blk
        xb = x_pad[tok_b]
        h = jax.nn.silu(xb @ w1[e]) * (xb @ w3[e])
        return acc.at[tok_b].add((h @ w2[e]) * w_b[:, None]), None

    acc, _ = lax.scan(body, jnp.zeros((T + 1, D), xt.dtype),
                      (buf_tok.reshape(n_blocks, EXPERT_BLOCK), buf_w.reshape(n_blocks, EXPERT_BLOCK), blk_e))
    return acc[:T]


def moe_ffn(x, lp):
    B, S, D = x.shape
    xt = x.reshape(B * S, D)
    scores = jax.nn.sigmoid((xt @ lp['router_w']).astype(jnp.float32))
    choice = scores + lp['router_bias'].astype(jnp.float32)
    grouped = choice.reshape(-1, N_GROUPS, N_EXPERTS // N_GROUPS)
    group_score = lax.top_k(grouped, 2)[0].sum(-1)
    _, gidx = lax.top_k(group_score, TOPK_GROUPS)
    gmask = jax.nn.one_hot(gidx, N_GROUPS).sum(-2) > 0
    emask = jnp.repeat(gmask, N_EXPERTS // N_GROUPS, axis=-1)
    _, idx = lax.top_k(jnp.where(emask, choice, -jnp.inf), TOP_K)
    w = jnp.take_along_axis(scores, idx, axis=-1)
    w = (w / jnp.sum(w, axis=-1, keepdims=True) * ROUTE_SCALE).astype(x.dtype)
    routed = routed_experts(xt, idx, w, lp['exp_w1'], lp['exp_w3'], lp['exp_w2'])
    shared = (jax.nn.silu(xt @ lp['sh_w1']) * (xt @ lp['sh_w3'])) @ lp['sh_w2']
    return (routed + shared).reshape(B, S, D)


def trunk_layer(x, lp, conv_prev, h0, s5_re0, s5_im0, ckv_past, kr_past):
    B, S, _ = x.shape
    past = ckv_past.shape[1]
    q_pos = past + jnp.arange(S, dtype=jnp.int32)
    k_pos = jnp.arange(past + S, dtype=jnp.int32)
    proj = x @ lp['w_in']
    xa, ga, cq, ckv, kr, uc, gates = jnp.split(proj, np.cumsum(IN_WIDTHS)[:-1].tolist(), axis=-1)
    y_a, conv_new, h_new = rglru_branch(xa, ga, conv_prev, h0, lp)
    y_b, ckv_new, kr_new = mla_branch(cq, ckv, kr, ckv_past, kr_past, q_pos, k_pos, lp)
    y_c, s5_re_new, s5_im_new = s5_branch(uc, s5_re0, s5_im0, lp)
    g_a, g_b, g_c = jnp.split(jax.nn.sigmoid(gates), N_BRANCH, axis=-1)
    merged = (g_a * (y_a @ lp['w_branch_a']) + g_b * (y_b @ lp['w_branch_b'])
              + g_c * (y_c @ lp['w_branch_c']))
    x = layer_norm(ALPHA * x + merged @ lp['w_out'] + lp['b_out'], lp['ln1_g'], lp['ln1_b'])
    x = layer_norm(ALPHA * x + moe_ffn(x, lp), lp['ln2_g'], lp['ln2_b'])
    return x, (ckv_new, kr_new, conv_new, h_new, s5_re_new, s5_im_new)


def setup_inputs(seed: int = 0) -> dict:
    key = jax.random.key(seed)
    ks = iter(jax.random.split(key, 64))
    f32 = jnp.float32
    L = DEPTH
    D = D_MODEL

    def nrm(shape, scale):
        return jax.random.normal(next(ks), shape, f32) * scale

    x_prompt = nrm((BATCH, SEQ, D), 1.0)
    x_sample = nrm((DEC_BATCH, DEC_SEQ, D), 1.0)
    cache_mla_ckv = nrm((L, DEC_BATCH, PAST_LEN, KV_LORA), 1.0)
    cache_mla_krope = nrm((L, DEC_BATCH, PAST_LEN, QK_ROPE), 1.0)
    state_lru_conv = nrm((L, DEC_BATCH, CONV_WIDTH - 1, LRU_WIDTH), 1.0)
    state_lru_h = nrm((L, DEC_BATCH, LRU_WIDTH), 0.5)
    state_s5_re = nrm((L, DEC_BATCH, S5_GROUPS, S5_STATE), 0.1)
    state_s5_im = nrm((L, DEC_BATCH, S5_GROUPS, S5_STATE), 0.1)
    ln_in_g = 1.0 + nrm((D,), 0.02)
    ln_in_b = nrm((D,), 0.02)
    w_in = nrm((L, D, N_IN), D ** -0.5)
    lru_conv_w = nrm((L, CONV_WIDTH, LRU_WIDTH), CONV_WIDTH ** -0.5)
    lru_conv_b = nrm((L, LRU_WIDTH), 0.02)
    lru_gate_a_w = nrm((L, LRU_BLOCKS, LRU_BLOCK, LRU_BLOCK), LRU_BLOCK ** -0.5)
    lru_gate_a_b = nrm((L, LRU_BLOCKS, LRU_BLOCK), 0.02)
    lru_gate_x_w = nrm((L, LRU_BLOCKS, LRU_BLOCK, LRU_BLOCK), LRU_BLOCK ** -0.5)
    lru_gate_x_b = nrm((L, LRU_BLOCKS, LRU_BLOCK), 0.02)
    a0 = jax.random.uniform(next(ks), (L, LRU_WIDTH), f32, minval=0.9, maxval=0.999)
    s0 = a0 ** (1.0 / LRU_C)
    lru_lambda = jnp.log(s0) - jnp.log1p(-s0)
    mla_q_norm_g = 1.0 + nrm((L, Q_LORA), 0.02)
    mla_w_uq = nrm((L, Q_LORA, MLA_HEADS, QK_NOPE + QK_ROPE), Q_LORA ** -0.5)
    mla_kv_norm_g = 1.0 + nrm((L, KV_LORA), 0.02)
    mla_w_uk = nrm((L, KV_LORA, MLA_HEADS, QK_NOPE), KV_LORA ** -0.5)
    mla_w_uv = nrm((L, KV_LORA, MLA_HEADS, V_HEAD), KV_LORA ** -0.5)
    n_idx = jnp.arange(S5_STATE, dtype=f32)
    s5_lam_re = -0.5 + nrm((L, S5_GROUPS, S5_STATE), 0.01)
    s5_lam_im = math.pi * n_idx + nrm((L, S5_GROUPS, S5_STATE), 0.01)
    s5_log_dt = jax.random.uniform(next(ks), (L, S5_GROUPS), f32,
                                   minval=math.log(DT_MIN), maxval=math.log(DT_MAX))
    s5_b_re = nrm((L, S5_GROUPS, S5_STATE, S5_GROUP), (2 * S5_GROUP) ** -0.5)
    s5_b_im = nrm((L, S5_GROUPS, S5_STATE, S5_GROUP), (2 * S5_GROUP) ** -0.5)
    s5_c_re = nrm((L, S5_GROUPS, S5_GROUP, S5_STATE), (2 * S5_STATE) ** -0.5)
    s5_c_im = nrm((L, S5_GROUPS, S5_GROUP, S5_STATE), (2 * S5_STATE) ** -0.5)
    s5_d = nrm((L, S5_WIDTH), 1.0)
    s5_glu_w = nrm((L, S5_WIDTH, S5_WIDTH), S5_WIDTH ** -0.5)
    s5_glu_b = nrm((L, S5_WIDTH), 0.02)
    w_branch_a = nrm((L, LRU_WIDTH, D), LRU_WIDTH ** -0.5)
    w_branch_b = nrm((L, MLA_HEADS * V_HEAD, D), (MLA_HEADS * V_HEAD) ** -0.5)
    w_branch_c = nrm((L, S5_WIDTH, D), S5_WIDTH ** -0.5)
    w_out = nrm((L, D, D), D ** -0.5 * BETA)
    b_out = nrm((L, D), 0.02)
    ln1_g = 1.0 + nrm((L, D), 0.02)
    ln1_b = nrm((L, D), 0.02)
    router_w = nrm((L, D, N_EXPERTS), D ** -0.5)
    router_bias = nrm((L, N_EXPERTS), 0.01)
    exp_w1 = nrm((L, N_EXPERTS, D, EXPERT_FF), D ** -0.5)
    exp_w3 = nrm((L, N_EXPERTS, D, EXPERT_FF), D ** -0.5)
    exp_w2 = nrm((L, N_EXPERTS, EXPERT_FF, D), EXPERT_FF ** -0.5 * BETA)
    sh_w1 = nrm((L, D, SHARED_FF), D ** -0.5)
    sh_w3 = nrm((L, D, SHARED_FF), D ** -0.5)
    sh_w2 = nrm((L, SHARED_FF, D), SHARED_FF ** -0.5 * BETA)
    ln2_g = 1.0 + nrm((L, D), 0.02)
    ln2_b = nrm((L, D), 0.02)
    return {
        'x_prompt': x_prompt, 'x_sample': x_sample,
        'cache_mla_ckv': cache_mla_ckv, 'cache_mla_krope': cache_mla_krope,
        'state_lru_conv': state_lru_conv, 'state_lru_h': state_lru_h,
        'state_s5_re': state_s5_re, 'state_s5_im': state_s5_im,
        'ln_in_g': ln_in_g, 'ln_in_b': ln_in_b, 'w_in': w_in,
        'lru_conv_w': lru_conv_w, 'lru_conv_b': lru_conv_b,
        'lru_gate_a_w': lru_gate_a_w, 'lru_gate_a_b': lru_gate_a_b,
        'lru_gate_x_w': lru_gate_x_w, 'lru_gate_x_b': lru_gate_x_b, 'lru_lambda': lru_lambda,
        'mla_q_norm_g': mla_q_norm_g, 'mla_w_uq': mla_w_uq, 'mla_kv_norm_g': mla_kv_norm_g,
        'mla_w_uk': mla_w_uk, 'mla_w_uv': mla_w_uv,
        's5_lam_re': s5_lam_re, 's5_lam_im': s5_lam_im, 's5_log_dt': s5_log_dt,
        's5_b_re': s5_b_re, 's5_b_im': s5_b_im, 's5_c_re': s5_c_re, 's5_c_im': s5_c_im,
        's5_d': s5_d, 's5_glu_w': s5_glu_w, 's5_glu_b': s5_glu_b,
        'w_branch_a': w_branch_a, 'w_branch_b': w_branch_b, 'w_branch_c': w_branch_c,
        'w_out': w_out, 'b_out': b_out, 'ln1_g': ln1_g, 'ln1_b': ln1_b,
        'router_w': router_w, 'router_bias': router_bias,
        'exp_w1': exp_w1, 'exp_w3': exp_w3, 'exp_w2': exp_w2,
        'sh_w1': sh_w1, 'sh_w3': sh_w3, 'sh_w2': sh_w2, 'ln2_g': ln2_g, 'ln2_b': ln2_b,
    }


def reference(x_prompt, x_sample, cache_mla_ckv, cache_mla_krope, state_lru_conv, state_lru_h,
              state_s5_re, state_s5_im, ln_in_g, ln_in_b, w_in, lru_conv_w, lru_conv_b,
              lru_gate_a_w, lru_gate_a_b, lru_gate_x_w, lru_gate_x_b, lru_lambda,
              mla_q_norm_g, mla_w_uq, mla_kv_norm_g, mla_w_uk, mla_w_uv,
              s5_lam_re, s5_lam_im, s5_log_dt, s5_b_re, s5_b_im, s5_c_re, s5_c_im,
              s5_d, s5_glu_w, s5_glu_b, w_branch_a, w_branch_b, w_branch_c, w_out, b_out,
              ln1_g, ln1_b, router_w, router_bias, exp_w1, exp_w3, exp_w2,
              sh_w1, sh_w3, sh_w2, ln2_g, ln2_b):
    xp = layer_norm(x_prompt, ln_in_g, ln_in_b)
    xs = layer_norm(x_sample, ln_in_g, ln_in_b)
    bp = x_prompt.shape[0]
    new_p = []
    new_s = []
    for l in range(DEPTH):
        lp = dict(
            w_in=w_in[l], lru_conv_w=lru_conv_w[l], lru_conv_b=lru_conv_b[l],
            lru_gate_a_w=lru_gate_a_w[l], lru_gate_a_b=lru_gate_a_b[l],
            lru_gate_x_w=lru_gate_x_w[l], lru_gate_x_b=lru_gate_x_b[l], lru_lambda=lru_lambda[l],
            mla_q_norm_g=mla_q_norm_g[l], mla_w_uq=mla_w_uq[l], mla_kv_norm_g=mla_kv_norm_g[l],
            mla_w_uk=mla_w_uk[l], mla_w_uv=mla_w_uv[l],
            s5_lam_re=s5_lam_re[l], s5_lam_im=s5_lam_im[l], s5_log_dt=s5_log_dt[l],
            s5_b_re=s5_b_re[l], s5_b_im=s5_b_im[l], s5_c_re=s5_c_re[l], s5_c_im=s5_c_im[l],
            s5_d=s5_d[l], s5_glu_w=s5_glu_w[l], s5_glu_b=s5_glu_b[l],
            w_branch_a=w_branch_a[l], w_branch_b=w_branch_b[l], w_branch_c=w_branch_c[l],
            w_out=w_out[l], b_out=b_out[l], ln1_g=ln1_g[l], ln1_b=ln1_b[l],
            router_w=router_w[l], router_bias=router_bias[l],
            exp_w1=exp_w1[l], exp_w3=exp_w3[l], exp_w2=exp_w2[l],
            sh_w1=sh_w1[l], sh_w3=sh_w3[l], sh_w2=sh_w2[l], ln2_g=ln2_g[l], ln2_b=ln2_b[l])
        xp, st_p = trunk_layer(
            xp, lp,
            jnp.zeros((bp, CONV_WIDTH - 1, LRU_WIDTH), xp.dtype),
            jnp.zeros((bp, LRU_WIDTH), jnp.float32),
            jnp.zeros((bp, S5_GROUPS, S5_STATE), jnp.float32),
            jnp.zeros((bp, S5_GROUPS, S5_STATE), jnp.float32),
            jnp.zeros((bp, 0, KV_LORA), xp.dtype),
            jnp.zeros((bp, 0, QK_ROPE), xp.dtype))
        xs, st_s = trunk_layer(
            xs, lp, state_lru_conv[l], state_lru_h[l], state_s5_re[l], state_s5_im[l],
            cache_mla_ckv[l], cache_mla_krope[l])
        new_p.append(st_p)
        new_s.append(st_s)
    p_ckv, p_krope, p_conv, p_h, p_s5_re, p_s5_im = [jnp.stack(t) for t in zip(*new_p)]
    s_ckv, s_krope, s_conv, s_h, s_s5_re, s_s5_im = [jnp.stack(t) for t in zip(*new_s)]
    return (xp, xs, p_ckv, p_krope, p_conv, p_h, p_s5_re, p_s5_im,
            s_ckv, s_krope, s_conv, s_h, s_s5_re, s_s5_im)
```

```python
import functools
import math

import jax
import jax.numpy as jnp
import numpy as np
from jax import lax
from jax.experimental import pallas as pl
from jax.experimental.pallas import tpu as pltpu

F32 = jnp.float32
BF16 = jnp.bfloat16
I32 = jnp.int32

D_MODEL = 2048
DEPTH = 2
CHUNK = 64
LN_EPS = 1e-5
RMS_EPS = 1e-6
ALPHA = (2 * DEPTH) ** 0.25
LRU_WIDTH = 1024
LRU_BLOCKS = 8
LRU_BLOCK = 128
CONV_WIDTH = 4
LRU_C = 8.0
MLA_HEADS = 16
QK_NOPE = 128
QK_ROPE = 64
QK_DIM = QK_NOPE + QK_ROPE
V_HEAD = 128
Q_LORA = 512
KV_LORA = 512
ROPE_THETA = 10000.0
S5_WIDTH = 1024
S5_GROUP = 16
S5_GROUPS = 64
S5_STATE = 64
S5_HID = S5_GROUPS * S5_STATE
N_EXPERTS = 64
TOP_K = 6
N_GROUPS = 8
GROUP_SIZE = N_EXPERTS // N_GROUPS
TOPK_GROUPS = 4
EXPERT_FF = 512
ROUTE_SCALE = 2.5

COL_XA, COL_GA, COL_CQ, COL_CKV, COL_UC, COL_GATES, COL_KR = 0, 1024, 2048, 2560, 3072, 4096, 10240
N_PROJ = 10368

V7X_VMEM_BYTES = 64 * 1024 * 1024
VMEM_LIMIT = V7X_VMEM_BYTES - 8 * 1024 * 1024
LANES = 128
SUBLANES = 8

EXPERT_ROWS = 256
ATTN_TQ = 512


def _cparams(sem):
    return pltpu.CompilerParams(dimension_semantics=sem, vmem_limit_bytes=VMEM_LIMIT)


def _row_tile(n, pref):
    for t in (512, 256, 128, 64, 32, 16, 8):
        if t <= pref and n % t == 0:
            return t
    raise ValueError(f"no row tile for {n}")


def _ln_rows(x, g, b):
    mu = jnp.mean(x, axis=-1, keepdims=True)
    xc = x - mu
    var = jnp.mean(xc * xc, axis=-1, keepdims=True)
    return xc * lax.rsqrt(var + LN_EPS) * g + b


def _dot(a, b):
    return jnp.dot(a, b, preferred_element_type=F32)


def _ln_in_kernel(xp_ref, xs_ref, g_ref, b_ref, o_ref, ob_ref, *, n_prompt_tiles):
    i = pl.program_id(0)

    def emit(x):
        y = _ln_rows(x, g_ref[...], b_ref[...])
        o_ref[...] = y
        ob_ref[...] = y.astype(BF16)

    @pl.when(i < n_prompt_tiles)
    def _():
        emit(xp_ref[...])

    @pl.when(i >= n_prompt_tiles)
    def _():
        emit(xs_ref[...])


def _ln_in(xp, xs, g, b):
    tp, ts = xp.shape[0], xs.shape[0]
    tm = _row_tile(math.gcd(tp, ts), 512)
    npt, nst = tp // tm, ts // tm
    t = tp + ts
    return pl.pallas_call(
        functools.partial(_ln_in_kernel, n_prompt_tiles=npt),
        grid=(npt + nst,),
        in_specs=[
            pl.BlockSpec((tm, D_MODEL), lambda i: (jnp.minimum(i, npt - 1), 0)),
            pl.BlockSpec((tm, D_MODEL), lambda i: (jnp.maximum(i - npt, 0), 0)),
            pl.BlockSpec((1, D_MODEL), lambda i: (0, 0)),
            pl.BlockSpec((1, D_MODEL), lambda i: (0, 0)),
        ],
        out_specs=[pl.BlockSpec((tm, D_MODEL), lambda i: (i, 0)),
                   pl.BlockSpec((tm, D_MODEL), lambda i: (i, 0))],
        out_shape=[jax.ShapeDtypeStruct((t, D_MODEL), F32), jax.ShapeDtypeStruct((t, D_MODEL), BF16)],
        compiler_params=_cparams(("parallel",)),
        name="ln_in",
    )(xp, xs, g.reshape(1, -1), b.reshape(1, -1))


def _mm_kernel(x_ref, w_ref, o_ref):
    o_ref[...] = _dot(x_ref[...], w_ref[...])


def _in_proj(xb, w):
    t, k = xb.shape
    n = w.shape[1]
    tm = _row_tile(t, 512)
    tn = 1152
    assert n % tn == 0
    return pl.pallas_call(
        _mm_kernel,
        grid=(n // tn, t // tm),
        in_specs=[pl.BlockSpec((tm, k), lambda j, i: (i, 0)),
                  pl.BlockSpec((k, tn), lambda j, i: (0, j))],
        out_specs=pl.BlockSpec((tm, tn), lambda j, i: (i, j)),
        out_shape=jax.ShapeDtypeStruct((t, n), F32),
        compiler_params=_cparams(("parallel", "parallel")),
        name="in_proj",
    )(xb, w)


def _lru_kernel(xa_ref, ga_ref, cs_ref, h0_ref, cw_ref, cb_ref, wg_ref, bg_ref, sp_ref,
                y_ref, cn_ref, hn_ref, xbuf, hcar, *, rows, n_prompt_tiles, tiles_per_seq):
    c = pl.program_id(0)
    first = jnp.logical_or(c >= n_prompt_tiles, c % tiles_per_seq == 0)

    @pl.when(first)
    def _():
        xbuf[5:8, :] = cs_ref[0]
        hcar[0:1, :] = h0_ref[0]

    xa = xa_ref[...]
    xbuf[8:8 + rows, :] = xa
    cw = cw_ref[...]
    xc = (cb_ref[...] + cw[0:1] * xbuf[5:5 + rows, :] + cw[1:2] * xbuf[6:6 + rows, :]
          + cw[2:3] * xbuf[7:7 + rows, :] + cw[3:4] * xa)
    tail = xa[rows - 3:rows, :]
    xbuf[5:8, :] = tail
    cn_ref[0] = tail

    row = lax.broadcasted_iota(I32, (rows, LRU_BLOCK), 0)
    for n in range(LRU_BLOCKS):
        sl = slice(n * LRU_BLOCK, (n + 1) * LRU_BLOCK)
        xcb = xc[:, sl]
        g = _dot(xcb.astype(BF16), wg_ref[n]) + bg_ref[n]
        r = jax.nn.sigmoid(g[:, :LRU_BLOCK])
        gi = jax.nn.sigmoid(g[:, LRU_BLOCK:])
        log_a = -LRU_C * r * sp_ref[:, sl]
        a = jnp.exp(log_a)
        th = jnp.tanh(log_a)
        b = jnp.sqrt(-2.0 * th / (1.0 - th)) * (gi * xcb)
        s = 1
        while s < rows:
            keep = row >= s
            a_s = jnp.where(keep, pltpu.roll(a, s, 0), 1.0)
            b_s = jnp.where(keep, pltpu.roll(b, s, 0), 0.0)
            b = a * b_s + b
            a = a * a_s
            s *= 2
        h = a * hcar[0:1, sl] + b
        hcar[0:1, sl] = h[rows - 1:rows, :]
        y_ref[:, sl] = (h * jax.nn.gelu(ga_ref[:, sl])).astype(BF16)
    hn_ref[0] = hcar[0:1, :]


def _lru_branch(proj, conv_state, h_state, conv_w, conv_b, wg, bg, sp, *, bp, sp_len, bs):
    t = proj.shape[0]
    rows = CHUNK
    tps = sp_len // rows
    npt = bp * tps
    nt = t // rows
    nseq = bp + bs
    w = LRU_WIDTH

    def seq_of(c):
        return jnp.where(c < npt, c // tps, bp + (c - npt))

    return pl.pallas_call(
        functools.partial(_lru_kernel, rows=rows, n_prompt_tiles=npt, tiles_per_seq=tps),
        grid=(nt,),
        in_specs=[
            pl.BlockSpec((rows, w), lambda c: (c, COL_XA // w)),
            pl.BlockSpec((rows, w), lambda c: (c, COL_GA // w)),
            pl.BlockSpec((1, 3, w), lambda c: (seq_of(c), 0, 0)),
            pl.BlockSpec((1, 1, w), lambda c: (seq_of(c), 0, 0)),
            pl.BlockSpec((CONV_WIDTH, w), lambda c: (0, 0)),
            pl.BlockSpec((1, w), lambda c: (0, 0)),
            pl.BlockSpec((LRU_BLOCKS, LRU_BLOCK, 2 * LRU_BLOCK), lambda c: (0, 0, 0)),
            pl.BlockSpec((LRU_BLOCKS, 1, 2 * LRU_BLOCK), lambda c: (0, 0, 0)),
            pl.BlockSpec((1, w), lambda c: (0, 0)),
        ],
        out_specs=[
            pl.BlockSpec((rows, w), lambda c: (c, 0)),
            pl.BlockSpec((1, 3, w), lambda c: (seq_of(c), 0, 0)),
            pl.BlockSpec((1, 1, w), lambda c: (seq_of(c), 0, 0)),
        ],
        out_shape=[jax.ShapeDtypeStruct((t, w), BF16),
                   jax.ShapeDtypeStruct((nseq, 3, w), F32),
                   jax.ShapeDtypeStruct((nseq, 1, w), F32)],
        scratch_shapes=[pltpu.VMEM((rows + 8, w), F32), pltpu.VMEM((SUBLANES, w), F32)],
        compiler_params=_cparams(("arbitrary",)),
        name="rglru",
    )(proj, proj, conv_state, h_state, conv_w, conv_b.reshape(1, -1), wg, bg, sp.reshape(1, -1))


S5_COLS = 512
S5_RG = 4


def _s5_kernel(u_ref, h0r_ref, h0i_ref, bbr_ref, bbi_ref, ccr_ref, cci_ref, ar_ref, ai_ref, d_ref,
               gw_ref, gb_ref, o_ref, hr_out, hi_out, bur, bui, hr, hi, *, nb, steps):
    i = pl.program_id(0)

    @pl.when(i == 0)
    def _():
        hr[...] = h0r_ref[...]
        hi[...] = h0i_ref[...]

    u = u_ref[...]
    ub = u.astype(BF16)
    kin = S5_WIDTH // S5_RG
    kst = S5_HID // S5_RG
    for r in range(S5_RG):
        ur = ub[:, r * kin:(r + 1) * kin]
        bur[:, r * kst:(r + 1) * kst] = _dot(ur, bbr_ref[r])
        bui[:, r * kst:(r + 1) * kst] = _dot(ur, bbi_ref[r])

    for cc in range(S5_HID // S5_COLS):
        cs = slice(cc * S5_COLS, (cc + 1) * S5_COLS)
        ar = jnp.broadcast_to(ar_ref[:, cs], (SUBLANES, S5_COLS))
        ai = jnp.broadcast_to(ai_ref[:, cs], (SUBLANES, S5_COLS))
        for bg in range(nb // SUBLANES):
            bsl = slice(bg * SUBLANES, (bg + 1) * SUBLANES)

            def body(t, carry, cs=cs, bg=bg, ar=ar, ai=ai):
                h_r, h_i = carry
                rsl = pl.ds(pl.multiple_of(t * nb + bg * SUBLANES, SUBLANES), SUBLANES)
                n_r = ar * h_r - ai * h_i + bur[rsl, cs]
                n_i = ar * h_i + ai * h_r + bui[rsl, cs]
                bur[rsl, cs] = n_r
                bui[rsl, cs] = n_i
                return n_r, n_i

            h_r, h_i = lax.fori_loop(0, steps, body, (hr[bsl, cs], hi[bsl, cs]))
            hr[bsl, cs] = h_r
            hi[bsl, cs] = h_i

    ys = []
    for r in range(S5_RG):
        hrb = bur[:, r * kst:(r + 1) * kst].astype(BF16)
        hib = bui[:, r * kst:(r + 1) * kst].astype(BF16)
        ys.append(_dot(hrb, ccr_ref[r]) - _dot(hib, cci_ref[r]))
    y = jnp.concatenate(ys, axis=-1) + d_ref[...] * u
    z = jax.nn.gelu(y)
    gate = jax.nn.sigmoid(_dot(z.astype(BF16), gw_ref[...]) + gb_ref[...])
    o_ref[...] = (z * gate).astype(BF16)
    hr_out[...] = hr[...]
    hi_out[...] = hi[...]


def _s5_branch(u_tm, h0r, h0i, pw, *, nb, seq_len):
    assert nb % SUBLANES == 0
    steps = max(1, min(seq_len, 256 // nb))
    while seq_len % steps:
        steps -= 1
    rows = steps * nb
    kin = S5_WIDTH // S5_RG
    kst = S5_HID // S5_RG
    const2 = lambda i: (0, 0)
    const3 = lambda i: (0, 0, 0)
    return pl.pallas_call(
        functools.partial(_s5_kernel, nb=nb, steps=steps),
        grid=(seq_len // steps,),
        in_specs=[
            pl.BlockSpec((rows, S5_WIDTH), lambda i: (i, 0)),
            pl.BlockSpec((nb, S5_HID), const2),
            pl.BlockSpec((nb, S5_HID), const2),
            pl.BlockSpec((S5_RG, kin, kst), const3),
            pl.BlockSpec((S5_RG, kin, kst), const3),
            pl.BlockSpec((S5_RG, kst, kin), const3),
            pl.BlockSpec((S5_RG, kst, kin), const3),
            pl.BlockSpec((1, S5_HID), const2),
            pl.BlockSpec((1, S5_HID), const2),
            pl.BlockSpec((1, S5_WIDTH), const2),
            pl.BlockSpec((S5_WIDTH, S5_WIDTH), const2),
            pl.BlockSpec((1, S5_WIDTH), const2),
        ],
        out_specs=[pl.BlockSpec((rows, S5_WIDTH), lambda i: (i, 0)),
                   pl.BlockSpec((nb, S5_HID), const2),
                   pl.BlockSpec((nb, S5_HID), const2)],
        out_shape=[jax.ShapeDtypeStruct((seq_len * nb, S5_WIDTH), BF16),
                   jax.ShapeDtypeStruct((nb, S5_HID), F32),
                   jax.ShapeDtypeStruct((nb, S5_HID), F32)],
        scratch_shapes=[pltpu.VMEM((rows, S5_HID), F32), pltpu.VMEM((rows, S5_HID), F32),
                        pltpu.VMEM((nb, S5_HID), F32), pltpu.VMEM((nb, S5_HID), F32)],
        compiler_params=_cparams(("arbitrary",)),
        name="s5",
    )(u_tm, h0r, h0i, pw["bbr"], pw["bbi"], pw["ccr"], pw["cci"], pw["ar"], pw["ai"], pw["d"],
      pw["glu_w"], pw["glu_b"])


def _rms_rows(x, g):
    return x * lax.rsqrt(jnp.mean(x * x, axis=-1, keepdims=True) + RMS_EPS) * g


def _qproj_kernel(cq_ref, g_ref, w_ref, cos_ref, sin_ref, q_ref):
    xn = _rms_rows(cq_ref[...], g_ref[...]).astype(BF16)
    cos2 = cos_ref[...]
    sin2 = sin_ref[...]
    for h in range(MLA_HEADS):
        r = _dot(xn, w_ref[h])
        q_ref[h, :, 0:QK_NOPE] = r[:, 0:QK_NOPE].astype(BF16)
        rot = r[:, QK_NOPE:QK_DIM] * cos2 + r[:, QK_DIM:QK_DIM + QK_ROPE] * sin2
        q_ref[h, :, QK_NOPE:QK_DIM] = rot.astype(BF16)


def _q_proj(proj, g, w_aug, cos2, sin2):
    t = proj.shape[0]
    tm = _row_tile(t, 256)
    return pl.pallas_call(
        _qproj_kernel,
        grid=(t // tm,),
        in_specs=[
            pl.BlockSpec((tm, Q_LORA), lambda i: (i, COL_CQ // Q_LORA)),
            pl.BlockSpec((1, Q_LORA), lambda i: (0, 0)),
            pl.BlockSpec((MLA_HEADS, Q_LORA, 2 * LANES), lambda i: (0, 0, 0)),
            pl.BlockSpec((tm, QK_ROPE), lambda i: (i, 0)),
            pl.BlockSpec((tm, QK_ROPE), lambda i: (i, 0)),
        ],
        out_specs=pl.BlockSpec((MLA_HEADS, tm, QK_DIM), lambda i: (0, i, 0)),
        out_shape=jax.ShapeDtypeStruct((MLA_HEADS, t, QK_DIM), BF16),
        compiler_params=_cparams(("parallel",)),
        name="mla_q_proj",
    )(proj, g.reshape(1, -1), w_aug, cos2, sin2)


def _kvproj_kernel(ckv_ref, kr_ref, g_ref, wk_ref, wv_ref, cos_ref, sin_ref,
                   ckvn_ref, krn_ref, k_ref, v_ref):
    xn = _rms_rows(ckv_ref[...], g_ref[...])
    ckvn_ref[...] = xn
    xb = xn.astype(BF16)
    kr = kr_ref[...]
    rot = kr[:, 0:QK_ROPE] * cos_ref[...] + kr[:, QK_ROPE:2 * QK_ROPE] * sin_ref[...]
    krn_ref[...] = rot
    rb = rot.astype(BF16)
    for h in range(MLA_HEADS):
        k_ref[h, :, 0:QK_NOPE] = _dot(xb, wk_ref[h]).astype(BF16)
        k_ref[h, :, QK_NOPE:QK_DIM] = rb
        v_ref[h] = _dot(xb, wv_ref[h]).astype(BF16)


def _kv_proj(proj, g, wk, wv, cos2, sin2):
    t = proj.shape[0]
    tm = _row_tile(t, 256)
    c3 = lambda i: (0, 0, 0)
    return pl.pallas_call(
        _kvproj_kernel,
        grid=(t // tm,),
        in_specs=[
            pl.BlockSpec((tm, KV_LORA), lambda i: (i, COL_CKV // KV_LORA)),
            pl.BlockSpec((tm, LANES), lambda i: (i, COL_KR // LANES)),
            pl.BlockSpec((1, KV_LORA), lambda i: (0, 0)),
            pl.BlockSpec((MLA_HEADS, KV_LORA, QK_NOPE), c3),
            pl.BlockSpec((MLA_HEADS, KV_LORA, V_HEAD), c3),
            pl.BlockSpec((tm, QK_ROPE), lambda i: (i, 0)),
            pl.BlockSpec((tm, QK_ROPE), lambda i: (i, 0)),
        ],
        out_specs=[
            pl.BlockSpec((tm, KV_LORA), lambda i: (i, 0)),
            pl.BlockSpec((tm, QK_ROPE), lambda i: (i, 0)),
            pl.BlockSpec((MLA_HEADS, tm, QK_DIM), lambda i: (0, i, 0)),
            pl.BlockSpec((MLA_HEADS, tm, V_HEAD), lambda i: (0, i, 0)),
        ],
        out_shape=[jax.ShapeDtypeStruct((t, KV_LORA), F32),
                   jax.ShapeDtypeStruct((t, QK_ROPE), F32),
                   jax.ShapeDtypeStruct((MLA_HEADS, t, QK_DIM), BF16),
                   jax.ShapeDtypeStruct((MLA_HEADS, t, V_HEAD), BF16)],
        compiler_params=_cparams(("parallel",)),
        name="mla_kv_proj",
    )(proj, proj, g.reshape(1, -1), wk, wv, cos2, sin2)


def _kvcache_kernel(ckv_ref, kr_ref, wk_ref, wv_ref, k_ref, v_ref):
    xb = ckv_ref[...].astype(BF16)
    rb = kr_ref[...].astype(BF16)
    for h in range(MLA_HEADS):
        k_ref[h, :, 0:QK_NOPE] = _dot(xb, wk_ref[h]).astype(BF16)
        k_ref[h, :, QK_NOPE:QK_DIM] = rb
        v_ref[h] = _dot(xb, wv_ref[h]).astype(BF16)


def _kv_cache_proj(ckv, kr, wk, wv):
    t = ckv.shape[0]
    tm = _row_tile(t, 256)
    c3 = lambda i: (0, 0, 0)
    return pl.pallas_call(
        _kvcache_kernel,
        grid=(t // tm,),
        in_specs=[
            pl.BlockSpec((tm, KV_LORA), lambda i: (i, 0)),
            pl.BlockSpec((tm, QK_ROPE), lambda i: (i, 0)),
            pl.BlockSpec((MLA_HEADS, KV_LORA, QK_NOPE), c3),
            pl.BlockSpec((MLA_HEADS, KV_LORA, V_HEAD), c3),
        ],
        out_specs=[pl.BlockSpec((MLA_HEADS, tm, QK_DIM), lambda i: (0, i, 0)),
                   pl.BlockSpec((MLA_HEADS, tm, V_HEAD), lambda i: (0, i, 0))],
        out_shape=[jax.ShapeDtypeStruct((MLA_HEADS, t, QK_DIM), BF16),
                   jax.ShapeDtypeStruct((MLA_HEADS, t, V_HEAD), BF16)],
        compiler_params=_cparams(("parallel",)),
        name="mla_kv_cache_proj",
    )(ckv, kr, wk, wv)


def _qk(q, k):
    return lax.dot_general(q, k, (((1,), (1,)), ((), ())), preferred_element_type=F32)


def _attn_prompt_kernel(q_ref, k_ref, v_ref, o_ref, *, tq, scale):
    qi = pl.program_id(2)
    q = q_ref[0]

    def step(ki, carry, diagonal):
        m, l, acc = carry
        ksl = pl.ds(pl.multiple_of(ki * tq, tq), tq)
        s = _qk(q, k_ref[0, ksl, :]) * scale
        if diagonal:
            rr = lax.broadcasted_iota(I32, (tq, tq), 0) // CHUNK
            cc = lax.broadcasted_iota(I32, (tq, tq), 1) // CHUNK
            s = jnp.where(cc <= rr, s, -jnp.inf)
        m_new = jnp.maximum(m, jnp.max(s, axis=-1, keepdims=True))
        alpha = jnp.exp(m - m_new)
        p = jnp.exp(s - m_new)
        l = alpha * l + jnp.sum(p, axis=-1, keepdims=True)
        acc = alpha * acc + _dot(p.astype(BF16), v_ref[0, ksl, :])
        return m_new, l, acc

    init = (jnp.full((tq, 1), -jnp.inf, F32), jnp.zeros((tq, 1), F32), jnp.zeros((tq, V_HEAD), F32))
    carry = lax.fori_loop(0, qi, lambda ki, c: step(ki, c, False), init)
    _, l, acc = step(qi, carry, True)
    o_ref[...] = (acc / l).astype(BF16)


def _attn_prompt(q, k, v, *, bp, sp_len):
    tq = _row_tile(sp_len, ATTN_TQ)
    nq = sp_len // tq
    scale = QK_DIM ** -0.5
    return pl.pallas_call(
        functools.partial(_attn_prompt_kernel, tq=tq, scale=scale),
        grid=(bp, MLA_HEADS, nq),
        in_specs=[
            pl.BlockSpec((1, tq, QK_DIM), lambda b, h, i: (h, b * nq + i, 0)),
            pl.BlockSpec((1, sp_len, QK_DIM), lambda b, h, i: (h, b, 0)),
            pl.BlockSpec((1, sp_len, V_HEAD), lambda b, h, i: (h, b, 0)),
        ],
        out_specs=pl.BlockSpec((tq, V_HEAD), lambda b, h, i: (b * nq + i, h)),
        out_shape=jax.ShapeDtypeStruct((bp * sp_len, MLA_HEADS * V_HEAD), BF16),
        compiler_params=_cparams(("parallel", "parallel", "arbitrary")),
        name="mla_attn_prompt",
    )(q, k, v)


def _attn_sample_kernel(q_ref, kp_ref, vp_ref, kn_ref, vn_ref, o_ref, *, scale):
    for h in range(MLA_HEADS):
        q = q_ref[h]
        s1 = _qk(q, kp_ref[h]) * scale
        s2 = _qk(q, kn_ref[h]) * scale
        m = jnp.maximum(jnp.max(s1, axis=-1, keepdims=True), jnp.max(s2, axis=-1, keepdims=True))
        p1 = jnp.exp(s1 - m)
        p2 = jnp.exp(s2 - m)
        l = jnp.sum(p1, axis=-1, keepdims=True) + jnp.sum(p2, axis=-1, keepdims=True)
        acc = _dot(p1.astype(BF16), vp_ref[h]) + _dot(p2.astype(BF16), vn_ref[h])
        o_ref[:, h * V_HEAD:(h + 1) * V_HEAD] = (acc / l).astype(BF16)


def _attn_sample(q, k, v, kpast, vpast, *, bs, ss_len, past, tp):
    assert ss_len == CHUNK and past % CHUNK == 0 and tp % ss_len == 0
    off = tp // ss_len
    scale = QK_DIM ** -0.5
    hh = MLA_HEADS
    return pl.pallas_call(
        functools.partial(_attn_sample_kernel, scale=scale),
        grid=(bs,),
        in_specs=[
            pl.BlockSpec((hh, ss_len, QK_DIM), lambda b: (0, off + b, 0)),
            pl.BlockSpec((hh, past, QK_DIM), lambda b: (0, b, 0)),
            pl.BlockSpec((hh, past, V_HEAD), lambda b: (0, b, 0)),
            pl.BlockSpec((hh, ss_len, QK_DIM), lambda b: (0, off + b, 0)),
            pl.BlockSpec((hh, ss_len, V_HEAD), lambda b: (0, off + b, 0)),
        ],
        out_specs=pl.BlockSpec((ss_len, hh * V_HEAD), lambda b: (b, 0)),
        out_shape=jax.ShapeDtypeStruct((bs * ss_len, hh * V_HEAD), BF16),
        compiler_params=_cparams(("parallel",)),
        name="mla_attn_sample",
    )(q, kpast, vpast, k, v)


def _merge_kernel(ya_ref, ybp_ref, ybs_ref, yc_ref, ga_ref, gb_ref, gc_ref, wa_ref, wb_ref, wc_ref, o_ref, *,
                  n_prompt_tiles):
    i = pl.program_id(1)

    def emit(yb):
        m = jax.nn.sigmoid(ga_ref[...]) * _dot(ya_ref[...], wa_ref[...])
        m = m + jax.nn.sigmoid(gb_ref[...]) * _dot(yb, wb_ref[...])
        m = m + jax.nn.sigmoid(gc_ref[...]) * _dot(yc_ref[...], wc_ref[...])
        o_ref[...] = m.astype(BF16)

    @pl.when(i < n_prompt_tiles)
    def _():
        emit(ybp_ref[...])

    @pl.when(i >= n_prompt_tiles)
    def _():
        emit(ybs_ref[...])


def _merge(ya, yb_p, yb_s, yc, proj, wa, wb, wc):
    t = ya.shape[0]
    tp, ts = yb_p.shape[0], yb_s.shape[0]
    tm = _row_tile(math.gcd(tp, ts), 256)
    npt = tp // tm
    tn = 1024
    nn = D_MODEL // tn
    g0 = COL_GATES // tn
    return pl.pallas_call(
        functools.partial(_merge_kernel, n_prompt_tiles=npt),
        grid=(nn, t // tm),
        in_specs=[
            pl.BlockSpec((tm, LRU_WIDTH), lambda j, i: (i, 0)),
            pl.BlockSpec((tm, MLA_HEADS * V_HEAD), lambda j, i: (jnp.minimum(i, npt - 1), 0)),
            pl.BlockSpec((tm, MLA_HEADS * V_HEAD), lambda j, i: (jnp.maximum(i - npt, 0), 0)),
            pl.BlockSpec((tm, S5_WIDTH), lambda j, i: (i, 0)),
            pl.BlockSpec((tm, tn), lambda j, i: (i, g0 + j)),
            pl.BlockSpec((tm, tn), lambda j, i: (i, g0 + nn + j)),
            pl.BlockSpec((tm, tn), lambda j, i: (i, g0 + 2 * nn + j)),
            pl.BlockSpec((LRU_WIDTH, tn), lambda j, i: (0, j)),
            pl.BlockSpec((MLA_HEADS * V_HEAD, tn), lambda j, i: (0, j)),
            pl.BlockSpec((S5_WIDTH, tn), lambda j, i: (0, j)),
        ],
        out_specs=pl.BlockSpec((tm, tn), lambda j, i: (i, j)),
        out_shape=jax.ShapeDtypeStruct((t, D_MODEL), BF16),
        compiler_params=_cparams(("parallel", "parallel")),
        name="branch_merge",
    )(ya, yb_p, yb_s, yc, proj, proj, proj, wa, wb, wc)


def _outln_kernel(x_ref, m_ref, w_ref, bo_ref, g_ref, b_ref, o_ref, ob_ref):
    y = ALPHA * x_ref[...] + _dot(m_ref[...], w_ref[...]) + bo_ref[...]
    y = _ln_rows(y, g_ref[...], b_ref[...])
    o_ref[...] = y
    ob_ref[...] = y.astype(BF16)


def _out_ln(x, merged, w_out, b_out, g, b):
    t = x.shape[0]
    tm = _row_tile(t, 256)
    c2 = lambda i: (0, 0)
    return pl.pallas_call(
        _outln_kernel,
        grid=(t // tm,),
        in_specs=[
            pl.BlockSpec((tm, D_MODEL), lambda i: (i, 0)),
            pl.BlockSpec((tm, D_MODEL), lambda i: (i, 0)),
            pl.BlockSpec((D_MODEL, D_MODEL), c2),
            pl.BlockSpec((1, D_MODEL), c2),
            pl.BlockSpec((1, D_MODEL), c2),
            pl.BlockSpec((1, D_MODEL), c2),
        ],
        out_specs=[pl.BlockSpec((tm, D_MODEL), lambda i: (i, 0)),
                   pl.BlockSpec((tm, D_MODEL), lambda i: (i, 0))],
        out_shape=[jax.ShapeDtypeStruct((t, D_MODEL), F32), jax.ShapeDtypeStruct((t, D_MODEL), BF16)],
        compiler_params=_cparams(("parallel",)),
        name="out_proj_ln1",
    )(x, merged, w_out, b_out.reshape(1, -1), g.reshape(1, -1), b.reshape(1, -1))


def _router_kernel(x_ref, wh_ref, wl_ref, bias_ref, idx_ref, wt_ref):
    tm = x_ref.shape[0]
    x = x_ref[...]
    xh = x.astype(BF16)
    xl = (x - xh.astype(F32)).astype(BF16)
    logits = _dot(xh, wh_ref[...]) + (_dot(xh, wl_ref[...]) + _dot(xl, wh_ref[...]))
    scores = jax.nn.sigmoid(logits)
    choice = scores + bias_ref[...]
    lane = lax.broadcasted_iota(I32, (tm, N_EXPERTS), 1)
    grp = lane // GROUP_SIZE
    neg = -jnp.inf

    def first_argmax(v):
        mx = jnp.max(v, axis=-1, keepdims=True)
        ix = jnp.min(jnp.where(v == mx, lane, N_EXPERTS), axis=-1, keepdims=True)
        return mx, ix

    gscore = []
    for g in range(N_GROUPS):
        v = jnp.where(grp == g, choice, neg)
        m1, i1 = first_argmax(v)
        m2 = jnp.max(jnp.where(lane == i1, neg, v), axis=-1, keepdims=True)
        gscore.append(m1 + m2)
    taken = [jnp.zeros((tm, 1), jnp.bool_) for _ in range(N_GROUPS)]
    for _ in range(TOPK_GROUPS):
        best = None
        for g in range(N_GROUPS):
            cand = jnp.where(taken[g], neg, gscore[g])
            best = cand if best is None else jnp.maximum(best, cand)
        found = jnp.zeros((tm, 1), jnp.bool_)
        for g in range(N_GROUPS):
            pick = jnp.logical_and(jnp.logical_and(gscore[g] == best, ~taken[g]), ~found)
            found = jnp.logical_or(found, pick)
            taken[g] = jnp.logical_or(taken[g], pick)
    emask = jnp.zeros((tm, N_EXPERTS), jnp.bool_)
    for g in range(N_GROUPS):
        emask = jnp.logical_or(emask, jnp.logical_and(grp == g, taken[g]))
    masked = jnp.where(emask, choice, neg)
    out_lane = lax.broadcasted_iota(I32, (tm, LANES), 1)
    idx_out = jnp.zeros((tm, LANES), I32)
    wt_out = jnp.zeros((tm, LANES), F32)
    wsum = jnp.zeros((tm, 1), F32)
    for k in range(TOP_K):
        _, ik = first_argmax(masked)
        hit = lane == ik
        wk = jnp.sum(jnp.where(hit, scores, 0.0), axis=-1, keepdims=True)
        masked = jnp.where(hit, neg, masked)
        wsum = wsum + wk
        idx_out = jnp.where(out_lane == k, ik, idx_out)
        wt_out = jnp.where(out_lane == k, wk, wt_out)
    idx_ref[...] = idx_out
    wt_ref[...] = wt_out / wsum * ROUTE_SCALE


def _router(x, w, bias):
    t = x.shape[0]
    tm = _row_tile(t, 256)
    wh = w.astype(BF16)
    wl = (w - wh.astype(F32)).astype(BF16)
    return pl.pallas_call(
        _router_kernel,
        grid=(t // tm,),
        in_specs=[pl.BlockSpec((tm, D_MODEL), lambda i: (i, 0)),
                  pl.BlockSpec((D_MODEL, N_EXPERTS), lambda i: (0, 0)),
                  pl.BlockSpec((D_MODEL, N_EXPERTS), lambda i: (0, 0)),
                  pl.BlockSpec((1, N_EXPERTS), lambda i: (0, 0))],
        out_specs=[pl.BlockSpec((tm, LANES), lambda i: (i, 0)),
                   pl.BlockSpec((tm, LANES), lambda i: (i, 0))],
        out_shape=[jax.ShapeDtypeStruct((t, LANES), I32), jax.ShapeDtypeStruct((t, LANES), F32)],
        compiler_params=_cparams(("parallel",)),
        name="moe_router",
    )(x, wh, wl, bias.reshape(1, -1))


def _expert_kernel(blk_e_ref, nvalid_ref, tok_ref, tokn_ref, slot_ref, x_hbm, w1_ref, w3_ref, w2_ref,
                   o_hbm, xbuf, ybuf, gsem, ssem, *, rows):
    del blk_e_ref
    b = pl.program_id(0)
    nb = pl.num_programs(0)
    cur = b % 2
    nxt = 1 - cur

    def gather_copy(tok, j, buf):
        return pltpu.make_async_copy(x_hbm.at[pl.ds(tok, 1), :], xbuf.at[buf, pl.ds(j, 1), :], gsem.at[buf])

    def scatter_copy(slot, j, buf):
        return pltpu.make_async_copy(ybuf.at[buf, pl.ds(j, 1), :], o_hbm.at[pl.ds(slot, 1), :], ssem.at[buf])

    def start_gather(tref, buf):
        def body(j, carry):
            gather_copy(tref[0, 0, j], j, buf).start()
            return carry
        lax.fori_loop(0, rows, body, 0)

    def wait_scatter(count, buf):
        def body(j, carry):
            scatter_copy(0, j, buf).wait()
            return carry
        lax.fori_loop(0, count, body, 0)

    @pl.when(b == 0)
    def _():
        start_gather(tok_ref, cur)

    @pl.when(b + 1 < nb)
    def _():
        start_gather(tokn_ref, nxt)

    def wait_gather(j, carry):
        gather_copy(0, j, cur).wait()
        return carry
    lax.fori_loop(0, rows, wait_gather, 0)

    @pl.when(b >= 2)
    def _():
        wait_scatter(nvalid_ref[jnp.maximum(b - 2, 0)], cur)

    x = xbuf[cur].astype(BF16)
    h = jax.nn.silu(_dot(x, w1_ref[0])) * _dot(x, w3_ref[0])
    ybuf[cur] = _dot(h.astype(BF16), w2_ref[0])

    nv = nvalid_ref[b]

    def start_scatter(j, carry):
        scatter_copy(slot_ref[0, 0, j], j, cur).start()
        return carry
    lax.fori_loop(0, nv, start_scatter, 0)

    @pl.when(b == nb - 1)
    def _():
        wait_scatter(nv, cur)

        @pl.when(nb >= 2)
        def _():
            wait_scatter(nvalid_ref[jnp.maximum(b - 1, 0)], nxt)


def _experts(x, w1, w3, w2, blk_e, nvalid, buf_tok, buf_slot, *, n_out_rows):
    rows = EXPERT_ROWS
    n_blocks = blk_e.shape[0]
    tok3 = buf_tok.reshape(n_blocks, 1, rows)
    slot3 = buf_slot.reshape(n_blocks, 1, rows)
    smem_blk = lambda f: pl.BlockSpec((1, 1, rows), f, memory_space=pltpu.SMEM)
    grid_spec = pltpu.PrefetchScalarGridSpec(
        num_scalar_prefetch=2,
        grid=(n_blocks,),
        in_specs=[
            smem_blk(lambda b, be, nv: (b, 0, 0)),
            smem_blk(lambda b, be, nv: (jnp.minimum(b + 1, n_blocks - 1), 0, 0)),
            smem_blk(lambda b, be, nv: (b, 0, 0)),
            pl.BlockSpec(memory_space=pl.ANY),
            pl.BlockSpec((1, D_MODEL, EXPERT_FF), lambda b, be, nv: (be[b], 0, 0)),
            pl.BlockSpec((1, D_MODEL, EXPERT_FF), lambda b, be, nv: (be[b], 0, 0)),
            pl.BlockSpec((1, EXPERT_FF, D_MODEL), lambda b, be, nv: (be[b], 0, 0)),
        ],
        out_specs=pl.BlockSpec(memory_space=pl.ANY),
        scratch_shapes=[pltpu.VMEM((2, rows, D_MODEL), F32), pltpu.VMEM((2, rows, D_MODEL), F32),
                        pltpu.SemaphoreType.DMA((2,)), pltpu.SemaphoreType.DMA((2,))],
    )
    return pl.pallas_call(
        functools.partial(_expert_kernel, rows=rows),
        grid_spec=grid_spec,
        out_shape=jax.ShapeDtypeStruct((n_out_rows, D_MODEL), F32),
        compiler_params=_cparams(("arbitrary",)),
        name="moe_experts",
    )(blk_e, nvalid, tok3, tok3, slot3, x, w1, w3, w2)


def _combine_kernel(x_ref, xb_ref, wt_ref, y0, y1, y2, y3, y4, y5, s1_ref, s3_ref, s2_ref, g_ref, b_ref,
                    o_ref, ob_ref):
    wt = wt_ref[...]
    routed = None
    for k, yk in enumerate((y0, y1, y2, y3, y4, y5)):
        term = yk[...] * wt[:, k:k + 1]
        routed = term if routed is None else routed + term
    xb = xb_ref[...]
    hs = jax.nn.silu(_dot(xb, s1_ref[...])) * _dot(xb, s3_ref[...])
    shared = _dot(hs.astype(BF16), s2_ref[...])
    y = _ln_rows(ALPHA * x_ref[...] + (routed + shared), g_ref[...], b_ref[...])
    o_ref[...] = y
    ob_ref[...] = y.astype(BF16)


def _combine(x, xb, wt, y6, s1, s3, s2, g, b):
    t = x.shape[0]
    tm = _row_tile(t, 128)
    nt = t // tm
    c2 = lambda i: (0, 0)
    yspecs = [pl.BlockSpec((tm, D_MODEL), (lambda i, k=k: (k * nt + i, 0))) for k in range(TOP_K)]
    return pl.pallas_call(
        _combine_kernel,
        grid=(nt,),
        in_specs=[
            pl.BlockSpec((tm, D_MODEL), lambda i: (i, 0)),
            pl.BlockSpec((tm, D_MODEL), lambda i: (i, 0)),
            pl.BlockSpec((tm, LANES), lambda i: (i, 0)),
            *yspecs,
            pl.BlockSpec((D_MODEL, EXPERT_FF), c2),
            pl.BlockSpec((D_MODEL, EXPERT_FF), c2),
            pl.BlockSpec((EXPERT_FF, D_MODEL), c2),
            pl.BlockSpec((1, D_MODEL), c2),
            pl.BlockSpec((1, D_MODEL), c2),
        ],
        out_specs=[pl.BlockSpec((tm, D_MODEL), lambda i: (i, 0)),
                   pl.BlockSpec((tm, D_MODEL), lambda i: (i, 0))],
        out_shape=[jax.ShapeDtypeStruct((t, D_MODEL), F32), jax.ShapeDtypeStruct((t, D_MODEL), BF16)],
        compiler_params=_cparams(("parallel",)),
        name="moe_combine_ln2",
    )(x, xb, wt, y6, y6, y6, y6, y6, y6, s1, s3, s2, g.reshape(1, -1), b.reshape(1, -1))


def _route_plan(idx, t):
    rows = EXPERT_ROWS
    tk = t * TOP_K
    n_blocks = -(-(tk + N_EXPERTS * (rows - 1)) // rows)
    n_rows = n_blocks * rows
    flat_e = idx.reshape(-1)
    order = jnp.argsort(flat_e).astype(I32)
    counts = jnp.zeros((N_EXPERTS,), I32).at[flat_e].add(1)
    padded = (counts + rows - 1) // rows * rows
    pad_end = jnp.cumsum(padded)
    pad_start = pad_end - padded
    start = jnp.cumsum(counts) - counts
    r = jnp.arange(n_rows, dtype=I32)
    e_r = jnp.minimum(jnp.sum(r[:, None] >= pad_end[None, :], axis=1), N_EXPERTS - 1).astype(I32)
    j = r - pad_start[e_r]
    valid = jnp.logical_and(j < counts[e_r], r < pad_end[N_EXPERTS - 1])
    src = jnp.clip(start[e_r] + j, 0, tk - 1)
    flat = order[src]
    tok = flat // TOP_K
    kk = flat - tok * TOP_K
    buf_tok = jnp.where(valid, tok, 0).astype(I32)
    buf_slot = jnp.where(valid, kk * t + tok, 0).astype(I32)
    nvalid = jnp.sum(valid.reshape(n_blocks, rows), axis=1).astype(I32)
    blk_e = e_r.reshape(n_blocks, rows)[:, 0]
    return blk_e, nvalid, buf_tok, buf_slot


def _prep_layer(l, p):
    f32 = F32
    w_in = p["w_in"][l]
    offs = np.cumsum((0, 1024, 1024, 512, 512, 64, 1024))
    xa, ga, cq, ckv, kr, uc, gates = (w_in[:, offs[0]:offs[1]], w_in[:, offs[1]:offs[2]], w_in[:, offs[2]:offs[3]],
                                      w_in[:, offs[3]:offs[4]], w_in[:, offs[4]:offs[5]], w_in[:, offs[5]:offs[6]],
                                      w_in[:, offs[6]:])
    half = QK_ROPE // 2
    kr_sw = jnp.concatenate([kr[:, half:], kr[:, :half]], axis=1)
    w_in_aug = jnp.concatenate([xa, ga, cq, ckv, uc, gates, kr, kr_sw], axis=1).astype(BF16)

    wuq = p["mla_w_uq"][l]
    rope_w = wuq[:, :, QK_NOPE:]
    wuq_aug = jnp.concatenate([wuq, rope_w[:, :, half:], rope_w[:, :, :half]], axis=2)
    wuq_aug = jnp.transpose(wuq_aug, (1, 0, 2)).astype(BF16)
    wuk = jnp.transpose(p["mla_w_uk"][l], (1, 0, 2)).astype(BF16)
    wuv = jnp.transpose(p["mla_w_uv"][l], (1, 0, 2)).astype(BF16)

    wg = jnp.concatenate([p["lru_gate_a_w"][l], p["lru_gate_x_w"][l]], axis=-1).astype(BF16)
    bg = jnp.concatenate([p["lru_gate_a_b"][l], p["lru_gate_x_b"][l]], axis=-1)[:, None, :]
    sp = jax.nn.softplus(-p["lru_lambda"][l].astype(f32))

    dt = jnp.exp(p["s5_log_dt"][l].astype(f32))[:, None]
    lam_re = p["s5_lam_re"][l].astype(f32)
    lam_im = p["s5_lam_im"][l].astype(f32)
    mag = jnp.exp(lam_re * dt)
    a_re = mag * jnp.cos(lam_im * dt)
    a_im = mag * jnp.sin(lam_im * dt)
    den = jnp.square(lam_re) + jnp.square(lam_im)
    f_re = ((a_re - 1.0) * lam_re + a_im * lam_im) / den
    f_im = (a_im * lam_re - (a_re - 1.0) * lam_im) / den
    b_re = p["s5_b_re"][l].astype(f32)
    b_im = p["s5_b_im"][l].astype(f32)
    bb_re = f_re[..., None] * b_re - f_im[..., None] * b_im
    bb_im = f_re[..., None] * b_im + f_im[..., None] * b_re
    gpr = S5_GROUPS // S5_RG
    eye = jnp.eye(gpr, dtype=f32)

    def in_bd(bb):
        bbr = bb.reshape(S5_RG, gpr, S5_STATE, S5_GROUP)
        return jnp.einsum("rgpc,gh->rgchp", bbr, eye).reshape(S5_RG, gpr * S5_GROUP, gpr * S5_STATE).astype(BF16)

    def out_bd(cm):
        cr = cm.astype(f32).reshape(S5_RG, gpr, S5_GROUP, S5_STATE)
        return jnp.einsum("rgcp,gh->rgphc", cr, eye).reshape(S5_RG, gpr * S5_STATE, gpr * S5_GROUP).astype(BF16)

    s5 = dict(bbr=in_bd(bb_re), bbi=in_bd(bb_im), ccr=out_bd(p["s5_c_re"][l]), cci=out_bd(p["s5_c_im"][l]),
              ar=a_re.reshape(1, -1), ai=a_im.reshape(1, -1), d=p["s5_d"][l].astype(f32).reshape(1, -1),
              glu_w=p["s5_glu_w"][l].astype(BF16), glu_b=p["s5_glu_b"][l].reshape(1, -1))
    return dict(
        w_in=w_in_aug, wuq=wuq_aug, wuk=wuk, wuv=wuv, wg=wg, bg=bg, sp=sp, s5=s5,
        wa=p["w_branch_a"][l].astype(BF16), wb=p["w_branch_b"][l].astype(BF16), wc=p["w_branch_c"][l].astype(BF16),
        w_out=p["w_out"][l].astype(BF16), router_w=p["router_w"][l].astype(F32),
        w1=p["exp_w1"][l].astype(BF16), w3=p["exp_w3"][l].astype(BF16), w2=p["exp_w2"][l].astype(BF16),
        s1=p["sh_w1"][l].astype(BF16), s3=p["sh_w3"][l].astype(BF16), s2=p["sh_w2"][l].astype(BF16),
    )


def _rope_tables(bp, sp_len, bs, ss_len, past):
    half = QK_ROPE // 2
    inv = ROPE_THETA ** (-jnp.arange(half, dtype=F32) / half)
    pos = jnp.concatenate([jnp.tile(jnp.arange(sp_len, dtype=I32), bp),
                           jnp.tile(past + jnp.arange(ss_len, dtype=I32), bs)])
    ang = pos.astype(F32)[:, None] * inv
    cos, sin = jnp.cos(ang), jnp.sin(ang)
    return jnp.concatenate([cos, cos], axis=1), jnp.concatenate([-sin, sin], axis=1)


def _to_time_major(x, nb, seq_len):
    return x.reshape(nb, seq_len, -1).transpose(1, 0, 2).reshape(seq_len * nb, -1)


def _from_time_major(x, nb, seq_len):
    return x.reshape(seq_len, nb, -1).transpose(1, 0, 2).reshape(nb * seq_len, -1)


def kernel(x_prompt, x_sample, cache_mla_ckv, cache_mla_krope, state_lru_conv, state_lru_h, state_s5_re, state_s5_im, ln_in_g, ln_in_b, w_in, lru_conv_w, lru_conv_b, lru_gate_a_w, lru_gate_a_b, lru_gate_x_w, lru_gate_x_b, lru_lambda, mla_q_norm_g, mla_w_uq, mla_kv_norm_g, mla_w_uk, mla_w_uv, s5_lam_re, s5_lam_im, s5_log_dt, s5_b_re, s5_b_im, s5_c_re, s5_c_im, s5_d, s5_glu_w, s5_glu_b, w_branch_a, w_branch_b, w_branch_c, w_out, b_out, ln1_g, ln1_b, router_w, router_bias, exp_w1, exp_w3, exp_w2, sh_w1, sh_w3, sh_w2, ln2_g, ln2_b):
    p = dict(w_in=w_in, lru_gate_a_w=lru_gate_a_w, lru_gate_a_b=lru_gate_a_b, lru_gate_x_w=lru_gate_x_w,
             lru_gate_x_b=lru_gate_x_b, lru_lambda=lru_lambda, mla_w_uq=mla_w_uq, mla_w_uk=mla_w_uk,
             mla_w_uv=mla_w_uv, s5_lam_re=s5_lam_re, s5_lam_im=s5_lam_im, s5_log_dt=s5_log_dt, s5_b_re=s5_b_re,
             s5_b_im=s5_b_im, s5_c_re=s5_c_re, s5_c_im=s5_c_im, s5_d=s5_d, s5_glu_w=s5_glu_w, s5_glu_b=s5_glu_b,
             w_branch_a=w_branch_a, w_branch_b=w_branch_b, w_branch_c=w_branch_c, w_out=w_out, router_w=router_w,
             exp_w1=exp_w1, exp_w3=exp_w3, exp_w2=exp_w2, sh_w1=sh_w1, sh_w3=sh_w3, sh_w2=sh_w2)
    bp, sp_len, d = x_prompt.shape
    bs, ss_len, _ = x_sample.shape
    past = cache_mla_ckv.shape[2]
    depth = w_in.shape[0]
    assert d == D_MODEL and sp_len % CHUNK == 0 and ss_len == CHUNK
    tp, ts = bp * sp_len, bs * ss_len
    t = tp + ts

    cos2, sin2 = _rope_tables(bp, sp_len, bs, ss_len, past)
    x, xb = _ln_in(x_prompt.reshape(tp, d), x_sample.reshape(ts, d), ln_in_g, ln_in_b)

    new_p, new_s = [], []
    for l in range(depth):
        lp = _prep_layer(l, p)
        proj = _in_proj(xb, lp["w_in"])

        conv0 = jnp.concatenate([jnp.zeros((bp, CONV_WIDTH - 1, LRU_WIDTH), F32), state_lru_conv[l]], axis=0)
        h0 = jnp.concatenate([jnp.zeros((bp, LRU_WIDTH), F32), state_lru_h[l]], axis=0)[:, None, :]
        y_a, conv_new, h_new = _lru_branch(proj, conv0, h0, lru_conv_w[l], lru_conv_b[l], lp["wg"], lp["bg"],
                                           lp["sp"], bp=bp, sp_len=sp_len, bs=bs)
        h_new = h_new[:, 0, :]

        q = _q_proj(proj, mla_q_norm_g[l], lp["wuq"], cos2, sin2)
        ckv_new, kr_new, k, v = _kv_proj(proj, mla_kv_norm_g[l], lp["wuk"], lp["wuv"], cos2, sin2)
        kpast, vpast = _kv_cache_proj(cache_mla_ckv[l].reshape(bs * past, KV_LORA),
                                      cache_mla_krope[l].reshape(bs * past, QK_ROPE), lp["wuk"], lp["wuv"])
        yb_p = _attn_prompt(q, k, v, bp=bp, sp_len=sp_len)
        yb_s = _attn_sample(q, k, v, kpast, vpast, bs=bs, ss_len=ss_len, past=past, tp=tp)

        uc = proj[:, COL_UC:COL_UC + S5_WIDTH]
        zero_state = jnp.zeros((bp, S5_HID), F32)
        o_p, s5r_p, s5i_p = _s5_branch(_to_time_major(uc[:tp], bp, sp_len), zero_state, zero_state, lp["s5"],
                                       nb=bp, seq_len=sp_len)
        o_s, s5r_s, s5i_s = _s5_branch(_to_time_major(uc[tp:], bs, ss_len), state_s5_re[l].reshape(bs, S5_HID),
                                       state_s5_im[l].reshape(bs, S5_HID), lp["s5"], nb=bs, seq_len=ss_len)
        y_c = jnp.concatenate([_from_time_major(o_p, bp, sp_len), _from_time_major(o_s, bs, ss_len)], axis=0)

        merged = _merge(y_a, yb_p, yb_s, y_c, proj, lp["wa"], lp["wb"], lp["wc"])
        x, xb = _out_ln(x, merged, lp["w_out"], b_out[l], ln1_g[l], ln1_b[l])

        idx128, wt128 = _router(x, lp["router_w"], router_bias[l])
        blk_e, nvalid, buf_tok, buf_slot = _route_plan(idx128[:, :TOP_K], t)
        y6 = _experts(x, lp["w1"], lp["w3"], lp["w2"], blk_e, nvalid, buf_tok, buf_slot, n_out_rows=TOP_K * t)
        x, xb = _combine(x, xb, wt128, y6, lp["s1"], lp["s3"], lp["s2"], ln2_g[l], ln2_b[l])

        new_p.append((ckv_new[:tp].reshape(bp, sp_len, KV_LORA), kr_new[:tp].reshape(bp, sp_len, QK_ROPE),
                      conv_new[:bp], h_new[:bp], s5r_p.reshape(bp, S5_GROUPS, S5_STATE),
                      s5i_p.reshape(bp, S5_GROUPS, S5_STATE)))
        new_s.append((ckv_new[tp:].reshape(bs, ss_len, KV_LORA), kr_new[tp:].reshape(bs, ss_len, QK_ROPE),
                      conv_new[bp:], h_new[bp:], s5r_s.reshape(bs, S5_GROUPS, S5_STATE),
                      s5i_s.reshape(bs, S5_GROUPS, S5_STATE)))

    p_out = [jnp.stack(z) for z in zip(*new_p)]
    s_out = [jnp.stack(z) for z in zip(*new_s)]
    return (x[:tp].reshape(bp, sp_len, d), x[tp:].reshape(bs, ss_len, d), *p_out, *s_out)
```

```python
import functools
import math

import jax
import jax.numpy as jnp
import numpy as np
from jax import lax
from jax.experimental import pallas as pl
from jax.experimental.pallas import tpu as pltpu

F32 = jnp.float32
BF16 = jnp.bfloat16
I32 = jnp.int32

D_MODEL = 2048
DEPTH = 2
CHUNK = 64
LN_EPS = 1e-5
RMS_EPS = 1e-6
ALPHA = (2 * DEPTH) ** 0.25
LRU_WIDTH = 1024
LRU_BLOCKS = 8
LRU_BLOCK = 128
CONV_WIDTH = 4
LRU_C = 8.0
MLA_HEADS = 16
QK_NOPE = 128
QK_ROPE = 64
QK_DIM = QK_NOPE + QK_ROPE
V_HEAD = 128
Q_LORA = 512
KV_LORA = 512
ROPE_THETA = 10000.0
S5_WIDTH = 1024
S5_GROUP = 16
S5_GROUPS = 64
S5_STATE = 64
S5_HID = S5_GROUPS * S5_STATE
N_EXPERTS = 64
TOP_K = 6
N_GROUPS = 8
GROUP_SIZE = N_EXPERTS // N_GROUPS
TOPK_GROUPS = 4
EXPERT_FF = 512
ROUTE_SCALE = 2.5

COL_XA, COL_GA, COL_CQ, COL_CKV, COL_UC, COL_GATES, COL_KR = 0, 1024, 2048, 2560, 3072, 4096, 10240
N_PROJ = 10368

V7X_VMEM_BYTES = 64 * 1024 * 1024
VMEM_LIMIT = V7X_VMEM_BYTES - 8 * 1024 * 1024
LANES = 128
SUBLANES = 8

EXPERT_ROWS = 256
ATTN_TQ = 512


def _cparams(sem):
    return pltpu.CompilerParams(dimension_semantics=sem, vmem_limit_bytes=VMEM_LIMIT)


def _row_tile(n, pref):
    for t in (512, 256, 128, 64, 32, 16, 8):
        if t <= pref and n % t == 0:
            return t
    raise ValueError(f"no row tile for {n}")


def _ln_rows(x, g, b):
    mu = jnp.mean(x, axis=-1, keepdims=True)
    xc = x - mu
    var = jnp.mean(xc * xc, axis=-1, keepdims=True)
    return xc * lax.rsqrt(var + LN_EPS) * g + b


def _dot(a, b):
    return jnp.dot(a, b, preferred_element_type=F32)


def _ln_in_kernel(xp_ref, xs_ref, g_ref, b_ref, o_ref, ob_ref, *, n_prompt_tiles):
    i = pl.program_id(0)

    def emit(x):
        y = _ln_rows(x, g_ref[...], b_ref[...])
        o_ref[...] = y
        ob_ref[...] = y.astype(BF16)

    @pl.when(i < n_prompt_tiles)
    def _():
        emit(xp_ref[...])

    @pl.when(i >= n_prompt_tiles)
    def _():
        emit(xs_ref[...])


def _ln_in(xp, xs, g, b):
    tp, ts = xp.shape[0], xs.shape[0]
    tm = _row_tile(math.gcd(tp, ts), 512)
    npt, nst = tp // tm, ts // tm
    t = tp + ts
    return pl.pallas_call(
        functools.partial(_ln_in_kernel, n_prompt_tiles=npt),
        grid=(npt + nst,),
        in_specs=[
            pl.BlockSpec((tm, D_MODEL), lambda i: (jnp.minimum(i, npt - 1), 0)),
            pl.BlockSpec((tm, D_MODEL), lambda i: (jnp.maximum(i - npt, 0), 0)),
            pl.BlockSpec((1, D_MODEL), lambda i: (0, 0)),
            pl.BlockSpec((1, D_MODEL), lambda i: (0, 0)),
        ],
        out_specs=[pl.BlockSpec((tm, D_MODEL), lambda i: (i, 0)),
                   pl.BlockSpec((tm, D_MODEL), lambda i: (i, 0))],
        out_shape=[jax.ShapeDtypeStruct((t, D_MODEL), F32), jax.ShapeDtypeStruct((t, D_MODEL), BF16)],
        compiler_params=_cparams(("parallel",)),
        name="ln_in",
    )(xp, xs, g.reshape(1, -1), b.reshape(1, -1))


def _mm_kernel(x_ref, w_ref, o_ref):
    o_ref[...] = _dot(x_ref[...], w_ref[...])


def _in_proj(xb, w):
    t, k = xb.shape
    n = w.shape[1]
    tm = _row_tile(t, 512)
    tn = 1152
    assert n % tn == 0
    return pl.pallas_call(
        _mm_kernel,
        grid=(n // tn, t // tm),
        in_specs=[pl.BlockSpec((tm, k), lambda j, i: (i, 0)),
                  pl.BlockSpec((k, tn), lambda j, i: (0, j))],
        out_specs=pl.BlockSpec((tm, tn), lambda j, i: (i, j)),
        out_shape=jax.ShapeDtypeStruct((t, n), F32),
        compiler_params=_cparams(("parallel", "parallel")),
        name="in_proj",
    )(xb, w)


def _lru_kernel(xa_ref, ga_ref, cs_ref, h0_ref, cw_ref, cb_ref, wg_ref, bg_ref, sp_ref,
                y_ref, cn_ref, hn_ref, xbuf, hcar, *, rows, n_prompt_tiles, tiles_per_seq):
    c = pl.program_id(0)
    first = jnp.logical_or(c >= n_prompt_tiles, c % tiles_per_seq == 0)

    @pl.when(first)
    def _():
        xbuf[5:8, :] = cs_ref[0]
        hcar[0:1, :] = h0_ref[0]

    xa = xa_ref[...]
    xbuf[8:8 + rows, :] = xa
    cw = cw_ref[...]
    xc = (cb_ref[...] + cw[0:1] * xbuf[5:5 + rows, :] + cw[1:2] * xbuf[6:6 + rows, :]
          + cw[2:3] * xbuf[7:7 + rows, :] + cw[3:4] * xa)
    tail = xa[rows - 3:rows, :]
    xbuf[5:8, :] = tail
    cn_ref[0] = tail

    row = lax.broadcasted_iota(I32, (rows, LRU_BLOCK), 0)
    for n in range(LRU_BLOCKS):
        sl = slice(n * LRU_BLOCK, (n + 1) * LRU_BLOCK)
        xcb = xc[:, sl]
        g = _dot(xcb.astype(BF16), wg_ref[n]) + bg_ref[n]
        r = jax.nn.sigmoid(g[:, :LRU_BLOCK])
        gi = jax.nn.sigmoid(g[:, LRU_BLOCK:])
        log_a = -LRU_C * r * sp_ref[:, sl]
        a = jnp.exp(log_a)
        th = jnp.tanh(log_a)
        b = jnp.sqrt(-2.0 * th / (1.0 - th)) * (gi * xcb)
        s = 1
        while s < rows:
            keep = row >= s
            a_s = jnp.where(keep, pltpu.roll(a, s, 0), 1.0)
            b_s = jnp.where(keep, pltpu.roll(b, s, 0), 0.0)
            b = a * b_s + b
            a = a * a_s
            s *= 2
        h = a * hcar[0:1, sl] + b
        hcar[0:1, sl] = h[rows - 1:rows, :]
        y_ref[:, sl] = (h * jax.nn.gelu(ga_ref[:, sl])).astype(BF16)
    hn_ref[0] = hcar[0:1, :]


def _lru_branch(proj, conv_state, h_state, conv_w, conv_b, wg, bg, sp, *, bp, sp_len, bs):
    t = proj.shape[0]
    rows = CHUNK
    tps = sp_len // rows
    npt = bp * tps
    nt = t // rows
    nseq = bp + bs
    w = LRU_WIDTH

    def seq_of(c):
        return jnp.where(c < npt, c // tps, bp + (c - npt))

    return pl.pallas_call(
        functools.partial(_lru_kernel, rows=rows, n_prompt_tiles=npt, tiles_per_seq=tps),
        grid=(nt,),
        in_specs=[
            pl.BlockSpec((rows, w), lambda c: (c, COL_XA // w)),
            pl.BlockSpec((rows, w), lambda c: (c, COL_GA // w)),
            pl.BlockSpec((1, 3, w), lambda c: (seq_of(c), 0, 0)),
            pl.BlockSpec((1, 1, w), lambda c: (seq_of(c), 0, 0)),
            pl.BlockSpec((CONV_WIDTH, w), lambda c: (0, 0)),
            pl.BlockSpec((1, w), lambda c: (0, 0)),
            pl.BlockSpec((LRU_BLOCKS, LRU_BLOCK, 2 * LRU_BLOCK), lambda c: (0, 0, 0)),
            pl.BlockSpec((LRU_BLOCKS, 1, 2 * LRU_BLOCK), lambda c: (0, 0, 0)),
            pl.BlockSpec((1, w), lambda c: (0, 0)),
        ],
        out_specs=[
            pl.BlockSpec((rows, w), lambda c: (c, 0)),
            pl.BlockSpec((1, 3, w), lambda c: (seq_of(c), 0, 0)),
            pl.BlockSpec((1, 1, w), lambda c: (seq_of(c), 0, 0)),
        ],
        out_shape=[jax.ShapeDtypeStruct((t, w), BF16),
                   jax.ShapeDtypeStruct((nseq, 3, w), F32),
                   jax.ShapeDtypeStruct((nseq, 1, w), F32)],
        scratch_shapes=[pltpu.VMEM((rows + 8, w), F32), pltpu.VMEM((SUBLANES, w), F32)],
        compiler_params=_cparams(("arbitrary",)),
        name="rglru",
    )(proj, proj, conv_state, h_state, conv_w, conv_b.reshape(1, -1), wg, bg, sp.reshape(1, -1))


S5_COLS = 512
S5_RG = 4


def _s5_kernel(u_ref, h0r_ref, h0i_ref, bbr_ref, bbi_ref, ccr_ref, cci_ref, ar_ref, ai_ref, d_ref,
               gw_ref, gb_ref, o_ref, hr_out, hi_out, bur, bui, hr, hi, *, nb, steps):
    i = pl.program_id(0)

    @pl.when(i == 0)
    def _():
        hr[...] = h0r_ref[...]
        hi[...] = h0i_ref[...]

    u = u_ref[...]
    ub = u.astype(BF16)
    kin = S5_WIDTH // S5_RG
    kst = S5_HID // S5_RG
    for r in range(S5_RG):
        ur = ub[:, r * kin:(r + 1) * kin]
        bur[:, r * kst:(r + 1) * kst] = _dot(ur, bbr_ref[r])
        bui[:, r * kst:(r + 1) * kst] = _dot(ur, bbi_ref[r])

    for cc in range(S5_HID // S5_COLS):
        cs = slice(cc * S5_COLS, (cc + 1) * S5_COLS)
        ar = jnp.broadcast_to(ar_ref[:, cs], (SUBLANES, S5_COLS))
        ai = jnp.broadcast_to(ai_ref[:, cs], (SUBLANES, S5_COLS))
        for bg in range(nb // SUBLANES):
            bsl = slice(bg * SUBLANES, (bg + 1) * SUBLANES)

            def body(t, carry, cs=cs, bg=bg, ar=ar, ai=ai):
                h_r, h_i = carry
                rsl = pl.ds(pl.multiple_of(t * nb + bg * SUBLANES, SUBLANES), SUBLANES)
                n_r = ar * h_r - ai * h_i + bur[rsl, cs]
                n_i = ar * h_i + ai * h_r + bui[rsl, cs]
                bur[rsl, cs] = n_r
                bui[rsl, cs] = n_i
                return n_r, n_i

            h_r, h_i = lax.fori_loop(0, steps, body, (hr[bsl, cs], hi[bsl, cs]))
            hr[bsl, cs] = h_r
            hi[bsl, cs] = h_i

    ys = []
    for r in range(S5_RG):
        hrb = bur[:, r * kst:(r + 1) * kst].astype(BF16)
        hib = bui[:, r * kst:(r + 1) * kst].astype(BF16)
        ys.append(_dot(hrb, ccr_ref[r]) - _dot(hib, cci_ref[r]))
    y = jnp.concatenate(ys, axis=-1) + d_ref[...] * u
    z = jax.nn.gelu(y)
    gate = jax.nn.sigmoid(_dot(z.astype(BF16), gw_ref[...]) + gb_ref[...])
    o_ref[...] = (z * gate).astype(BF16)
    hr_out[...] = hr[...]
    hi_out[...] = hi[...]


def _s5_branch(u_tm, h0r, h0i, pw, *, nb, seq_len):
    assert nb % SUBLANES == 0
    steps = max(1, min(seq_len, 256 // nb))
    while seq_len % steps:
        steps -= 1
    rows = steps * nb
    kin = S5_WIDTH // S5_RG
    kst = S5_HID // S5_RG
    const2 = lambda i: (0, 0)
    const3 = lambda i: (0, 0, 0)
    return pl.pallas_call(
        functools.partial(_s5_kernel, nb=nb, steps=steps),
        grid=(seq_len // steps,),
        in_specs=[
            pl.BlockSpec((rows, S5_WIDTH), lambda i: (i, 0)),
            pl.BlockSpec((nb, S5_HID), const2),
            pl.BlockSpec((nb, S5_HID), const2),
            pl.BlockSpec((S5_RG, kin, kst), const3),
            pl.BlockSpec((S5_RG, kin, kst), const3),
            pl.BlockSpec((S5_RG, kst, kin), const3),
            pl.BlockSpec((S5_RG, kst, kin), const3),
            pl.BlockSpec((1, S5_HID), const2),
            pl.BlockSpec((1, S5_HID), const2),
            pl.BlockSpec((1, S5_WIDTH), const2),
            pl.BlockSpec((S5_WIDTH, S5_WIDTH), const2),
            pl.BlockSpec((1, S5_WIDTH), const2),
        ],
        out_specs=[pl.BlockSpec((rows, S5_WIDTH), lambda i: (i, 0)),
                   pl.BlockSpec((nb, S5_HID), const2),
                   pl.BlockSpec((nb, S5_HID), const2)],
        out_shape=[jax.ShapeDtypeStruct((seq_len * nb, S5_WIDTH), BF16),
                   jax.ShapeDtypeStruct((nb, S5_HID), F32),
                   jax.ShapeDtypeStruct((nb, S5_HID), F32)],
        scratch_shapes=[pltpu.VMEM((rows, S5_HID), F32), pltpu.VMEM((rows, S5_HID), F32),
                        pltpu.VMEM((nb, S5_HID), F32), pltpu.VMEM((nb, S5_HID), F32)],
        compiler_params=_cparams(("arbitrary",)),
        name="s5",
    )(u_tm, h0r, h0i, pw["bbr"], pw["bbi"], pw["ccr"], pw["cci"], pw["ar"], pw["ai"], pw["d"],
      pw["glu_w"], pw["glu_b"])


def _rms_rows(x, g):
    return x * lax.rsqrt(jnp.mean(x * x, axis=-1, keepdims=True) + RMS_EPS) * g


def _qproj_kernel(cq_ref, g_ref, w_ref, cos_ref, sin_ref, q_ref):
    xn = _rms_rows(cq_ref[...], g_ref[...]).astype(BF16)
    cos2 = cos_ref[...]
    sin2 = sin_ref[...]
    for h in range(MLA_HEADS):
        r = _dot(xn, w_ref[h])
        q_ref[h, :, 0:QK_NOPE] = r[:, 0:QK_NOPE].astype(BF16)
        rot = r[:, QK_NOPE:QK_DIM] * cos2 + r[:, QK_DIM:QK_DIM + QK_ROPE] * sin2
        q_ref[h, :, QK_NOPE:QK_DIM] = rot.astype(BF16)


def _q_proj(proj, g, w_aug, cos2, sin2):
    t = proj.shape[0]
    tm = _row_tile(t, 256)
    return pl.pallas_call(
        _qproj_kernel,
        grid=(t // tm,),
        in_specs=[
            pl.BlockSpec((tm, Q_LORA), lambda i: (i, COL_CQ // Q_LORA)),
            pl.BlockSpec((1, Q_LORA), lambda i: (0, 0)),
            pl.BlockSpec((MLA_HEADS, Q_LORA, 2 * LANES), lambda i: (0, 0, 0)),
            pl.BlockSpec((tm, QK_ROPE), lambda i: (i, 0)),
            pl.BlockSpec((tm, QK_ROPE), lambda i: (i, 0)),
        ],
        out_specs=pl.BlockSpec((MLA_HEADS, tm, QK_DIM), lambda i: (0, i, 0)),
        out_shape=jax.ShapeDtypeStruct((MLA_HEADS, t, QK_DIM), BF16),
        compiler_params=_cparams(("parallel",)),
        name="mla_q_proj",
    )(proj, g.reshape(1, -1), w_aug, cos2, sin2)


def _kvproj_kernel(ckv_ref, kr_ref, g_ref, wk_ref, wv_ref, cos_ref, sin_ref,
                   ckvn_ref, krn_ref, k_ref, v_ref):
    xn = _rms_rows(ckv_ref[...], g_ref[...])
    ckvn_ref[...] = xn
    xb = xn.astype(BF16)
    kr = kr_ref[...]
    rot = kr[:, 0:QK_ROPE] * cos_ref[...] + kr[:, QK_ROPE:2 * QK_ROPE] * sin_ref[...]
    krn_ref[...] = rot
    rb = rot.astype(BF16)
    for h in range(MLA_HEADS):
        k_ref[h, :, 0:QK_NOPE] = _dot(xb, wk_ref[h]).astype(BF16)
        k_ref[h, :, QK_NOPE:QK_DIM] = rb
        v_ref[h] = _dot(xb, wv_ref[h]).astype(BF16)


def _kv_proj(proj, g, wk, wv, cos2, sin2):
    t = proj.shape[0]
    tm = _row_tile(t, 256)
    c3 = lambda i: (0, 0, 0)
    return pl.pallas_call(
        _kvproj_kernel,
        grid=(t // tm,),
        in_specs=[
            pl.BlockSpec((tm, KV_LORA), lambda i: (i, COL_CKV // KV_LORA)),
            pl.BlockSpec((tm, LANES), lambda i: (i, COL_KR // LANES)),
            pl.BlockSpec((1, KV_LORA), lambda i: (0, 0)),
            pl.BlockSpec((MLA_HEADS, KV_LORA, QK_NOPE), c3),
            pl.BlockSpec((MLA_HEADS, KV_LORA, V_HEAD), c3),
            pl.BlockSpec((tm, QK_ROPE), lambda i: (i, 0)),
            pl.BlockSpec((tm, QK_ROPE), lambda i: (i, 0)),
        ],
        out_specs=[
            pl.BlockSpec((tm, KV_LORA), lambda i: (i, 0)),
            pl.BlockSpec((tm, QK_ROPE), lambda i: (i, 0)),
            pl.BlockSpec((MLA_HEADS, tm, QK_DIM), lambda i: (0, i, 0)),
            pl.BlockSpec((MLA_HEADS, tm, V_HEAD), lambda i: (0, i, 0)),
        ],
        out_shape=[jax.ShapeDtypeStruct((t, KV_LORA), F32),
                   jax.ShapeDtypeStruct((t, QK_ROPE), F32),
                   jax.ShapeDtypeStruct((MLA_HEADS, t, QK_DIM), BF16),
                   jax.ShapeDtypeStruct((MLA_HEADS, t, V_HEAD), BF16)],
        compiler_params=_cparams(("parallel",)),
        name="mla_kv_proj",
    )(proj, proj, g.reshape(1, -1), wk, wv, cos2, sin2)


def _kvcache_kernel(ckv_ref, kr_ref, wk_ref, wv_ref, k_ref, v_ref):
    xb = ckv_ref[...].astype(BF16)
    rb = kr_ref[...].astype(BF16)
    for h in range(MLA_HEADS):
        k_ref[h, :, 0:QK_NOPE] = _dot(xb, wk_ref[h]).astype(BF16)
        k_ref[h, :, QK_NOPE:QK_DIM] = rb
        v_ref[h] = _dot(xb, wv_ref[h]).astype(BF16)


def _kv_cache_proj(ckv, kr, wk, wv):
    t = ckv.shape[0]
    tm = _row_tile(t, 256)
    c3 = lambda i: (0, 0, 0)
    return pl.pallas_call(
        _kvcache_kernel,
        grid=(t // tm,),
        in_specs=[
            pl.BlockSpec((tm, KV_LORA), lambda i: (i, 0)),
            pl.BlockSpec((tm, QK_ROPE), lambda i: (i, 0)),
            pl.BlockSpec((MLA_HEADS, KV_LORA, QK_NOPE), c3),
            pl.BlockSpec((MLA_HEADS, KV_LORA, V_HEAD), c3),
        ],
        out_specs=[pl.BlockSpec((MLA_HEADS, tm, QK_DIM), lambda i: (0, i, 0)),
                   pl.BlockSpec((MLA_HEADS, tm, V_HEAD), lambda i: (0, i, 0))],
        out_shape=[jax.ShapeDtypeStruct((MLA_HEADS, t, QK_DIM), BF16),
                   jax.ShapeDtypeStruct((MLA_HEADS, t, V_HEAD), BF16)],
        compiler_params=_cparams(("parallel",)),
        name="mla_kv_cache_proj",
    )(ckv, kr, wk, wv)


def _qk(q, k):
    return lax.dot_general(q, k, (((1,), (1,)), ((), ())), preferred_element_type=F32)


def _attn_prompt_kernel(q_ref, k_ref, v_ref, o_ref, *, tq, scale):
    qi = pl.program_id(2)
    c = scale * math.log2(math.e)
    q = q_ref[0]

    def kv_rows(ki):
        return pl.ds(pl.multiple_of(ki * tq, tq), tq)

    def scores(ki):
        return _qk(q, k_ref[0, kv_rows(ki), :])

    def update(state, s, ki):
        m, l, acc = state
        m_new = jnp.maximum(m, jnp.max(s, axis=-1, keepdims=True))
        alpha = jnp.exp2((m - m_new) * c)
        p = jnp.exp2((s - m_new) * c)
        l = alpha * l + jnp.sum(p, axis=-1, keepdims=True)
        acc = alpha * acc + _dot(p.astype(BF16), v_ref[0, kv_rows(ki), :])
        return m_new, l, acc

    init = (jnp.full((tq, 1), -jnp.inf, F32), jnp.zeros((tq, 1), F32), jnp.zeros((tq, V_HEAD), F32))
    state = lax.fori_loop(0, qi, lambda ki, st: update(st, scores(ki), ki), init)
    rr = lax.broadcasted_iota(I32, (tq, tq), 0) // CHUNK
    cc = lax.broadcasted_iota(I32, (tq, tq), 1) // CHUNK
    _, l, acc = update(state, jnp.where(cc <= rr, scores(qi), -jnp.inf), qi)
    o_ref[...] = (acc / l).astype(BF16)


def _attn_prompt(q, k, v, *, bp, sp_len):
    tq = _row_tile(sp_len, ATTN_TQ)
    nq = sp_len // tq
    scale = QK_DIM ** -0.5
    return pl.pallas_call(
        functools.partial(_attn_prompt_kernel, tq=tq, scale=scale),
        grid=(bp, MLA_HEADS, nq),
        in_specs=[
            pl.BlockSpec((1, tq, QK_DIM), lambda b, h, i: (h, b * nq + i, 0)),
            pl.BlockSpec((1, sp_len, QK_DIM), lambda b, h, i: (h, b, 0)),
            pl.BlockSpec((1, sp_len, V_HEAD), lambda b, h, i: (h, b, 0)),
        ],
        out_specs=pl.BlockSpec((tq, V_HEAD), lambda b, h, i: (b * nq + i, h)),
        out_shape=jax.ShapeDtypeStruct((bp * sp_len, MLA_HEADS * V_HEAD), BF16),
        compiler_params=_cparams(("parallel", "parallel", "arbitrary")),
        name="mla_attn_prompt",
    )(q, k, v)


def _attn_sample_kernel(q_ref, kp_ref, vp_ref, kn_ref, vn_ref, o_ref, *, scale):
    for h in range(MLA_HEADS):
        q = q_ref[h]
        s1 = _qk(q, kp_ref[h]) * scale
        s2 = _qk(q, kn_ref[h]) * scale
        m = jnp.maximum(jnp.max(s1, axis=-1, keepdims=True), jnp.max(s2, axis=-1, keepdims=True))
        p1 = jnp.exp(s1 - m)
        p2 = jnp.exp(s2 - m)
        l = jnp.sum(p1, axis=-1, keepdims=True) + jnp.sum(p2, axis=-1, keepdims=True)
        acc = _dot(p1.astype(BF16), vp_ref[h]) + _dot(p2.astype(BF16), vn_ref[h])
        o_ref[:, h * V_HEAD:(h + 1) * V_HEAD] = (acc / l).astype(BF16)


def _attn_sample(q, k, v, kpast, vpast, *, bs, ss_len, past, tp):
    assert ss_len == CHUNK and past % CHUNK == 0 and tp % ss_len == 0
    off = tp // ss_len
    scale = QK_DIM ** -0.5
    hh = MLA_HEADS
    return pl.pallas_call(
        functools.partial(_attn_sample_kernel, scale=scale),
        grid=(bs,),
        in_specs=[
            pl.BlockSpec((hh, ss_len, QK_DIM), lambda b: (0, off + b, 0)),
            pl.BlockSpec((hh, past, QK_DIM), lambda b: (0, b, 0)),
            pl.BlockSpec((hh, past, V_HEAD), lambda b: (0, b, 0)),
            pl.BlockSpec((hh, ss_len, QK_DIM), lambda b: (0, off + b, 0)),
            pl.BlockSpec((hh, ss_len, V_HEAD), lambda b: (0, off + b, 0)),
        ],
        out_specs=pl.BlockSpec((ss_len, hh * V_HEAD), lambda b: (b, 0)),
        out_shape=jax.ShapeDtypeStruct((bs * ss_len, hh * V_HEAD), BF16),
        compiler_params=_cparams(("parallel",)),
        name="mla_attn_sample",
    )(q, kpast, vpast, k, v)


def _merge_kernel(ya_ref, ybp_ref, ybs_ref, yc_ref, ga_ref, gb_ref, gc_ref, wa_ref, wb_ref, wc_ref, o_ref, *,
                  n_prompt_tiles):
    i = pl.program_id(1)

    def emit(yb):
        m = jax.nn.sigmoid(ga_ref[...]) * _dot(ya_ref[...], wa_ref[...])
        m = m + jax.nn.sigmoid(gb_ref[...]) * _dot(yb, wb_ref[...])
        m = m + jax.nn.sigmoid(gc_ref[...]) * _dot(yc_ref[...], wc_ref[...])
        o_ref[...] = m.astype(BF16)

    @pl.when(i < n_prompt_tiles)
    def _():
        emit(ybp_ref[...])

    @pl.when(i >= n_prompt_tiles)
    def _():
        emit(ybs_ref[...])


def _merge(ya, yb_p, yb_s, yc, proj, wa, wb, wc):
    t = ya.shape[0]
    tp, ts = yb_p.shape[0], yb_s.shape[0]
    tm = _row_tile(math.gcd(tp, ts), 256)
    npt = tp // tm
    tn = 1024
    nn = D_MODEL // tn
    g0 = COL_GATES // tn
    return pl.pallas_call(
        functools.partial(_merge_kernel, n_prompt_tiles=npt),
        grid=(nn, t // tm),
        in_specs=[
            pl.BlockSpec((tm, LRU_WIDTH), lambda j, i: (i, 0)),
            pl.BlockSpec((tm, MLA_HEADS * V_HEAD), lambda j, i: (jnp.minimum(i, npt - 1), 0)),
            pl.BlockSpec((tm, MLA_HEADS * V_HEAD), lambda j, i: (jnp.maximum(i - npt, 0), 0)),
            pl.BlockSpec((tm, S5_WIDTH), lambda j, i: (i, 0)),
            pl.BlockSpec((tm, tn), lambda j, i: (i, g0 + j)),
            pl.BlockSpec((tm, tn), lambda j, i: (i, g0 + nn + j)),
            pl.BlockSpec((tm, tn), lambda j, i: (i, g0 + 2 * nn + j)),
            pl.BlockSpec((LRU_WIDTH, tn), lambda j, i: (0, j)),
            pl.BlockSpec((MLA_HEADS * V_HEAD, tn), lambda j, i: (0, j)),
            pl.BlockSpec((S5_WIDTH, tn), lambda j, i: (0, j)),
        ],
        out_specs=pl.BlockSpec((tm, tn), lambda j, i: (i, j)),
        out_shape=jax.ShapeDtypeStruct((t, D_MODEL), BF16),
        compiler_params=_cparams(("parallel", "parallel")),
        name="branch_merge",
    )(ya, yb_p, yb_s, yc, proj, proj, proj, wa, wb, wc)


def _outln_kernel(x_ref, m_ref, w_ref, bo_ref, g_ref, b_ref, o_ref, ob_ref):
    y = ALPHA * x_ref[...] + _dot(m_ref[...], w_ref[...]) + bo_ref[...]
    y = _ln_rows(y, g_ref[...], b_ref[...])
    o_ref[...] = y
    ob_ref[...] = y.astype(BF16)


def _out_ln(x, merged, w_out, b_out, g, b):
    t = x.shape[0]
    tm = _row_tile(t, 256)
    c2 = lambda i: (0, 0)
    return pl.pallas_call(
        _outln_kernel,
        grid=(t // tm,),
        in_specs=[
            pl.BlockSpec((tm, D_MODEL), lambda i: (i, 0)),
            pl.BlockSpec((tm, D_MODEL), lambda i: (i, 0)),
            pl.BlockSpec((D_MODEL, D_MODEL), c2),
            pl.BlockSpec((1, D_MODEL), c2),
            pl.BlockSpec((1, D_MODEL), c2),
            pl.BlockSpec((1, D_MODEL), c2),
        ],
        out_specs=[pl.BlockSpec((tm, D_MODEL), lambda i: (i, 0)),
                   pl.BlockSpec((tm, D_MODEL), lambda i: (i, 0))],
        out_shape=[jax.ShapeDtypeStruct((t, D_MODEL), F32), jax.ShapeDtypeStruct((t, D_MODEL), BF16)],
        compiler_params=_cparams(("parallel",)),
        name="out_proj_ln1",
    )(x, merged, w_out, b_out.reshape(1, -1), g.reshape(1, -1), b.reshape(1, -1))


def _router_kernel(x_ref, wh_ref, wl_ref, bias_ref, idx_ref, wt_ref):
    tm = x_ref.shape[0]
    x = x_ref[...]
    xh = x.astype(BF16)
    xl = (x - xh.astype(F32)).astype(BF16)
    logits = _dot(xh, wh_ref[...]) + (_dot(xh, wl_ref[...]) + _dot(xl, wh_ref[...]))
    scores = jax.nn.sigmoid(logits)
    choice = scores + bias_ref[...]
    lane = lax.broadcasted_iota(I32, (tm, N_EXPERTS), 1)
    grp = lane // GROUP_SIZE
    neg = -jnp.inf

    def first_argmax(v):
        mx = jnp.max(v, axis=-1, keepdims=True)
        ix = jnp.min(jnp.where(v == mx, lane, N_EXPERTS), axis=-1, keepdims=True)
        return mx, ix

    gscore = []
    for g in range(N_GROUPS):
        v = jnp.where(grp == g, choice, neg)
        m1, i1 = first_argmax(v)
        m2 = jnp.max(jnp.where(lane == i1, neg, v), axis=-1, keepdims=True)
        gscore.append(m1 + m2)
    taken = [jnp.zeros((tm, 1), jnp.bool_) for _ in range(N_GROUPS)]
    for _ in range(TOPK_GROUPS):
        best = None
        for g in range(N_GROUPS):
            cand = jnp.where(taken[g], neg, gscore[g])
            best = cand if best is None else jnp.maximum(best, cand)
        found = jnp.zeros((tm, 1), jnp.bool_)
        for g in range(N_GROUPS):
            pick = jnp.logical_and(jnp.logical_and(gscore[g] == best, ~taken[g]), ~found)
            found = jnp.logical_or(found, pick)
            taken[g] = jnp.logical_or(taken[g], pick)
    emask = jnp.zeros((tm, N_EXPERTS), jnp.bool_)
    for g in range(N_GROUPS):
        emask = jnp.logical_or(emask, jnp.logical_and(grp == g, taken[g]))
    masked = jnp.where(emask, choice, neg)
    out_lane = lax.broadcasted_iota(I32, (tm, LANES), 1)
    idx_out = jnp.zeros((tm, LANES), I32)
    wt_out = jnp.zeros((tm, LANES), F32)
    wsum = jnp.zeros((tm, 1), F32)
    for k in range(TOP_K):
        _, ik = first_argmax(masked)
        hit = lane == ik
        wk = jnp.sum(jnp.where(hit, scores, 0.0), axis=-1, keepdims=True)
        masked = jnp.where(hit, neg, masked)
        wsum = wsum + wk
        idx_out = jnp.where(out_lane == k, ik, idx_out)
        wt_out = jnp.where(out_lane == k, wk, wt_out)
    idx_ref[...] = idx_out
    wt_ref[...] = wt_out / wsum * ROUTE_SCALE


def _router(x, w, bias):
    t = x.shape[0]
    tm = _row_tile(t, 256)
    wh = w.astype(BF16)
    wl = (w - wh.astype(F32)).astype(BF16)
    return pl.pallas_call(
        _router_kernel,
        grid=(t // tm,),
        in_specs=[pl.BlockSpec((tm, D_MODEL), lambda i: (i, 0)),
                  pl.BlockSpec((D_MODEL, N_EXPERTS), lambda i: (0, 0)),
                  pl.BlockSpec((D_MODEL, N_EXPERTS), lambda i: (0, 0)),
                  pl.BlockSpec((1, N_EXPERTS), lambda i: (0, 0))],
        out_specs=[pl.BlockSpec((tm, LANES), lambda i: (i, 0)),
                   pl.BlockSpec((tm, LANES), lambda i: (i, 0))],
        out_shape=[jax.ShapeDtypeStruct((t, LANES), I32), jax.ShapeDtypeStruct((t, LANES), F32)],
        compiler_params=_cparams(("parallel",)),
        name="moe_router",
    )(x, wh, wl, bias.reshape(1, -1))


def _expert_kernel(blk_e_ref, nvalid_ref, tok_ref, tokn_ref, slot_ref, x_hbm, w1_ref, w3_ref, w2_ref,
                   o_hbm, xbuf, ybuf, w1b, w3b, w2b, gsem, ssem, *, rows):
    b = pl.program_id(0)
    nb = pl.num_programs(0)
    cur = b % 2
    nxt = 1 - cur

    def start_gather(tref, buf):
        def body(j, carry):
            pltpu.make_async_copy(x_hbm.at[pl.ds(tref[0, 0, j], 1), :], xbuf.at[buf, pl.ds(j, 1), :],
                                  gsem.at[buf]).start()
            return carry
        lax.fori_loop(0, rows, body, 0, unroll=8)

    def wait_gather(buf):
        pltpu.make_async_copy(x_hbm.at[pl.ds(0, rows), :], xbuf.at[buf], gsem.at[buf]).wait()

    def scatter_row(j, slot, buf):
        return pltpu.make_async_copy(ybuf.at[buf, pl.ds(j, 1), :], o_hbm.at[pl.ds(slot, 1), :], ssem.at[buf])

    def start_scatter(count, buf):
        def body(j, carry):
            scatter_row(j, slot_ref[0, 0, j], buf).start()
            return carry

        @pl.when(count == rows)
        def _():
            lax.fori_loop(0, rows, body, 0, unroll=8)

        @pl.when(count < rows)
        def _():
            lax.fori_loop(0, count, body, 0)

    def wait_scatter(count, buf):
        @pl.when(count == rows)
        def _():
            pltpu.make_async_copy(ybuf.at[buf], o_hbm.at[pl.ds(0, rows), :], ssem.at[buf]).wait()

        @pl.when(count < rows)
        def _():
            def body(j, carry):
                scatter_row(0, 0, buf).wait()
                return carry
            lax.fori_loop(0, count, body, 0)

    @pl.when(b == 0)
    def _():
        start_gather(tok_ref, cur)

    @pl.when(b + 1 < nb)
    def _():
        start_gather(tokn_ref, nxt)

    wait_gather(cur)

    @pl.when(b >= 2)
    def _():
        wait_scatter(nvalid_ref[jnp.maximum(b - 2, 0)], cur)

    @pl.when(jnp.logical_or(b == 0, blk_e_ref[b] != blk_e_ref[jnp.maximum(b - 1, 0)]))
    def _():
        w1b[...] = w1_ref[0].astype(BF16)
        w3b[...] = w3_ref[0].astype(BF16)
        w2b[...] = w2_ref[0].astype(BF16)

    nv = nvalid_ref[b]

    @pl.when(nv > 0)
    def _():
        x = xbuf[cur].astype(BF16)
        h = jax.nn.silu(_dot(x, w1b[...])) * _dot(x, w3b[...])
        ybuf[cur] = _dot(h.astype(BF16), w2b[...])
        start_scatter(nv, cur)

    @pl.when(b == nb - 1)
    def _():
        wait_scatter(nv, cur)

        @pl.when(b >= 1)
        def _():
            wait_scatter(nvalid_ref[jnp.maximum(b - 1, 0)], nxt)


def _experts(x, w1, w3, w2, blk_e, nvalid, buf_tok, buf_slot, *, n_out_rows):
    rows = EXPERT_ROWS
    n_blocks = blk_e.shape[0]
    tok3 = buf_tok.reshape(n_blocks, 1, rows)
    slot3 = buf_slot.reshape(n_blocks, 1, rows)
    smem_blk = lambda f: pl.BlockSpec((1, 1, rows), f, memory_space=pltpu.SMEM)
    grid_spec = pltpu.PrefetchScalarGridSpec(
        num_scalar_prefetch=2,
        grid=(n_blocks,),
        in_specs=[
            smem_blk(lambda b, be, nv: (b, 0, 0)),
            smem_blk(lambda b, be, nv: (jnp.minimum(b + 1, n_blocks - 1), 0, 0)),
            smem_blk(lambda b, be, nv: (b, 0, 0)),
            pl.BlockSpec(memory_space=pl.ANY),
            pl.BlockSpec((1, D_MODEL, EXPERT_FF), lambda b, be, nv: (be[b], 0, 0)),
            pl.BlockSpec((1, D_MODEL, EXPERT_FF), lambda b, be, nv: (be[b], 0, 0)),
            pl.BlockSpec((1, EXPERT_FF, D_MODEL), lambda b, be, nv: (be[b], 0, 0)),
        ],
        out_specs=pl.BlockSpec(memory_space=pl.ANY),
        scratch_shapes=[pltpu.VMEM((2, rows, D_MODEL), F32), pltpu.VMEM((2, rows, D_MODEL), F32),
                        pltpu.VMEM((D_MODEL, EXPERT_FF), BF16), pltpu.VMEM((D_MODEL, EXPERT_FF), BF16),
                        pltpu.VMEM((EXPERT_FF, D_MODEL), BF16),
                        pltpu.SemaphoreType.DMA((2,)), pltpu.SemaphoreType.DMA((2,))],
    )
    return pl.pallas_call(
        functools.partial(_expert_kernel, rows=rows),
        grid_spec=grid_spec,
        out_shape=jax.ShapeDtypeStruct((n_out_rows, D_MODEL), F32),
        compiler_params=_cparams(("arbitrary",)),
        name="moe_experts",
    )(blk_e, nvalid, tok3, tok3, slot3, x, w1, w3, w2)


def _combine_kernel(x_ref, xb_ref, wt_ref, y0, y1, y2, y3, y4, y5, s1_ref, s3_ref, s2_ref, g_ref, b_ref,
                    o_ref, ob_ref):
    wt = wt_ref[...]
    routed = None
    for k, yk in enumerate((y0, y1, y2, y3, y4, y5)):
        term = yk[...] * wt[:, k:k + 1]
        routed = term if routed is None else routed + term
    xb = xb_ref[...]
    hs = jax.nn.silu(_dot(xb, s1_ref[...])) * _dot(xb, s3_ref[...])
    shared = _dot(hs.astype(BF16), s2_ref[...])
    y = _ln_rows(ALPHA * x_ref[...] + (routed + shared), g_ref[...], b_ref[...])
    o_ref[...] = y
    ob_ref[...] = y.astype(BF16)


def _combine(x, xb, wt, y6, s1, s3, s2, g, b):
    t = x.shape[0]
    tm = _row_tile(t, 128)
    nt = t // tm
    c2 = lambda i: (0, 0)
    yspecs = [pl.BlockSpec((tm, D_MODEL), (lambda i, k=k: (k * nt + i, 0))) for k in range(TOP_K)]
    return pl.pallas_call(
        _combine_kernel,
        grid=(nt,),
        in_specs=[
            pl.BlockSpec((tm, D_MODEL), lambda i: (i, 0)),
            pl.BlockSpec((tm, D_MODEL), lambda i: (i, 0)),
            pl.BlockSpec((tm, LANES), lambda i: (i, 0)),
            *yspecs,
            pl.BlockSpec((D_MODEL, EXPERT_FF), c2),
            pl.BlockSpec((D_MODEL, EXPERT_FF), c2),
            pl.BlockSpec((EXPERT_FF, D_MODEL), c2),
            pl.BlockSpec((1, D_MODEL), c2),
            pl.BlockSpec((1, D_MODEL), c2),
        ],
        out_specs=[pl.BlockSpec((tm, D_MODEL), lambda i: (i, 0)),
                   pl.BlockSpec((tm, D_MODEL), lambda i: (i, 0))],
        out_shape=[jax.ShapeDtypeStruct((t, D_MODEL), F32), jax.ShapeDtypeStruct((t, D_MODEL), BF16)],
        compiler_params=_cparams(("parallel",)),
        name="moe_combine_ln2",
    )(x, xb, wt, y6, y6, y6, y6, y6, y6, s1, s3, s2, g.reshape(1, -1), b.reshape(1, -1))


def _route_plan(idx, t):
    rows = EXPERT_ROWS
    tk = t * TOP_K
    n_blocks = -(-(tk + N_EXPERTS * (rows - 1)) // rows)
    n_rows = n_blocks * rows
    flat_e = idx.reshape(-1)
    order = jnp.argsort(flat_e).astype(I32)
    counts = jnp.zeros((N_EXPERTS,), I32).at[flat_e].add(1)
    padded = (counts + rows - 1) // rows * rows
    pad_end = jnp.cumsum(padded)
    pad_start = pad_end - padded
    start = jnp.cumsum(counts) - counts
    blk = jnp.arange(n_blocks, dtype=I32)
    blk_e = jnp.minimum(jnp.sum(blk[:, None] * rows >= pad_end[None, :], axis=1), N_EXPERTS - 1).astype(I32)
    in_blk = jnp.arange(rows, dtype=I32)[None, :]
    j = blk[:, None] * rows + in_blk - pad_start[blk_e][:, None]
    valid = jnp.logical_and(j < counts[blk_e][:, None], (blk * rows < pad_end[N_EXPERTS - 1])[:, None])
    src = jnp.clip(start[blk_e][:, None] + j, 0, tk - 1)
    flat = order[src]
    tok = flat // TOP_K
    kk = flat - tok * TOP_K
    buf_tok = jnp.where(valid, tok, 0).astype(I32).reshape(-1)
    buf_slot = jnp.where(valid, kk * t + tok, 0).astype(I32).reshape(-1)
    nvalid = jnp.sum(valid, axis=1).astype(I32)
    return blk_e, nvalid, buf_tok, buf_slot


def _prep_layer(l, p):
    f32 = F32
    w_in = p["w_in"][l]
    offs = np.cumsum((0, 1024, 1024, 512, 512, 64, 1024))
    xa, ga, cq, ckv, kr, uc, gates = (w_in[:, offs[0]:offs[1]], w_in[:, offs[1]:offs[2]], w_in[:, offs[2]:offs[3]],
                                      w_in[:, offs[3]:offs[4]], w_in[:, offs[4]:offs[5]], w_in[:, offs[5]:offs[6]],
                                      w_in[:, offs[6]:])
    half = QK_ROPE // 2
    kr_sw = jnp.concatenate([kr[:, half:], kr[:, :half]], axis=1)
    w_in_aug = jnp.concatenate([xa, ga, cq, ckv, uc, gates, kr, kr_sw], axis=1).astype(BF16)

    wuq = p["mla_w_uq"][l]
    rope_w = wuq[:, :, QK_NOPE:]
    wuq_aug = jnp.concatenate([wuq, rope_w[:, :, half:], rope_w[:, :, :half]], axis=2)
    wuq_aug = jnp.transpose(wuq_aug, (1, 0, 2)).astype(BF16)
    wuk = jnp.transpose(p["mla_w_uk"][l], (1, 0, 2)).astype(BF16)
    wuv = jnp.transpose(p["mla_w_uv"][l], (1, 0, 2)).astype(BF16)

    wg = jnp.concatenate([p["lru_gate_a_w"][l], p["lru_gate_x_w"][l]], axis=-1).astype(BF16)
    bg = jnp.concatenate([p["lru_gate_a_b"][l], p["lru_gate_x_b"][l]], axis=-1)[:, None, :]
    sp = jax.nn.softplus(-p["lru_lambda"][l].astype(f32))

    dt = jnp.exp(p["s5_log_dt"][l].astype(f32))[:, None]
    lam_re = p["s5_lam_re"][l].astype(f32)
    lam_im = p["s5_lam_im"][l].astype(f32)
    mag = jnp.exp(lam_re * dt)
    a_re = mag * jnp.cos(lam_im * dt)
    a_im = mag * jnp.sin(lam_im * dt)
    den = jnp.square(lam_re) + jnp.square(lam_im)
    f_re = ((a_re - 1.0) * lam_re + a_im * lam_im) / den
    f_im = (a_im * lam_re - (a_re - 1.0) * lam_im) / den
    b_re = p["s5_b_re"][l].astype(f32)
    b_im = p["s5_b_im"][l].astype(f32)
    bb_re = f_re[..., None] * b_re - f_im[..., None] * b_im
    bb_im = f_re[..., None] * b_im + f_im[..., None] * b_re
    gpr = S5_GROUPS // S5_RG
    eye = jnp.eye(gpr, dtype=f32)

    def in_bd(bb):
        bbr = bb.reshape(S5_RG, gpr, S5_STATE, S5_GROUP)
        return jnp.einsum("rgpc,gh->rgchp", bbr, eye).reshape(S5_RG, gpr * S5_GROUP, gpr * S5_STATE).astype(BF16)

    def out_bd(cm):
        cr = cm.astype(f32).reshape(S5_RG, gpr, S5_GROUP, S5_STATE)
        return jnp.einsum("rgcp,gh->rgphc", cr, eye).reshape(S5_RG, gpr * S5_STATE, gpr * S5_GROUP).astype(BF16)

    s5 = dict(bbr=in_bd(bb_re), bbi=in_bd(bb_im), ccr=out_bd(p["s5_c_re"][l]), cci=out_bd(p["s5_c_im"][l]),
              ar=a_re.reshape(1, -1), ai=a_im.reshape(1, -1), d=p["s5_d"][l].astype(f32).reshape(1, -1),
              glu_w=p["s5_glu_w"][l].astype(BF16), glu_b=p["s5_glu_b"][l].reshape(1, -1))
    return dict(
        w_in=w_in_aug, wuq=wuq_aug, wuk=wuk, wuv=wuv, wg=wg, bg=bg, sp=sp, s5=s5,
        wa=p["w_branch_a"][l].astype(BF16), wb=p["w_branch_b"][l].astype(BF16), wc=p["w_branch_c"][l].astype(BF16),
        w_out=p["w_out"][l].astype(BF16), router_w=p["router_w"][l].astype(F32),
        w1=p["exp_w1"][l], w3=p["exp_w3"][l], w2=p["exp_w2"][l],
        s1=p["sh_w1"][l].astype(BF16), s3=p["sh_w3"][l].astype(BF16), s2=p["sh_w2"][l].astype(BF16),
    )


def _rope_tables(bp, sp_len, bs, ss_len, past):
    half = QK_ROPE // 2
    inv = ROPE_THETA ** (-jnp.arange(half, dtype=F32) / half)
    pos = jnp.concatenate([jnp.tile(jnp.arange(sp_len, dtype=I32), bp),
                           jnp.tile(past + jnp.arange(ss_len, dtype=I32), bs)])
    ang = pos.astype(F32)[:, None] * inv
    cos, sin = jnp.cos(ang), jnp.sin(ang)
    return jnp.concatenate([cos, cos], axis=1), jnp.concatenate([-sin, sin], axis=1)


def _to_time_major(x, nb, seq_len):
    return x.reshape(nb, seq_len, -1).transpose(1, 0, 2).reshape(seq_len * nb, -1)


def _from_time_major(x, nb, seq_len):
    return x.reshape(seq_len, nb, -1).transpose(1, 0, 2).reshape(nb * seq_len, -1)


def kernel(x_prompt, x_sample, cache_mla_ckv, cache_mla_krope, state_lru_conv, state_lru_h, state_s5_re, state_s5_im, ln_in_g, ln_in_b, w_in, lru_conv_w, lru_conv_b, lru_gate_a_w, lru_gate_a_b, lru_gate_x_w, lru_gate_x_b, lru_lambda, mla_q_norm_g, mla_w_uq, mla_kv_norm_g, mla_w_uk, mla_w_uv, s5_lam_re, s5_lam_im, s5_log_dt, s5_b_re, s5_b_im, s5_c_re, s5_c_im, s5_d, s5_glu_w, s5_glu_b, w_branch_a, w_branch_b, w_branch_c, w_out, b_out, ln1_g, ln1_b, router_w, router_bias, exp_w1, exp_w3, exp_w2, sh_w1, sh_w3, sh_w2, ln2_g, ln2_b):
    p = dict(w_in=w_in, lru_gate_a_w=lru_gate_a_w, lru_gate_a_b=lru_gate_a_b, lru_gate_x_w=lru_gate_x_w,
             lru_gate_x_b=lru_gate_x_b, lru_lambda=lru_lambda, mla_w_uq=mla_w_uq, mla_w_uk=mla_w_uk,
             mla_w_uv=mla_w_uv, s5_lam_re=s5_lam_re, s5_lam_im=s5_lam_im, s5_log_dt=s5_log_dt, s5_b_re=s5_b_re,
             s5_b_im=s5_b_im, s5_c_re=s5_c_re, s5_c_im=s5_c_im, s5_d=s5_d, s5_glu_w=s5_glu_w, s5_glu_b=s5_glu_b,
             w_branch_a=w_branch_a, w_branch_b=w_branch_b, w_branch_c=w_branch_c, w_out=w_out, router_w=router_w,
             exp_w1=exp_w1, exp_w3=exp_w3, exp_w2=exp_w2, sh_w1=sh_w1, sh_w3=sh_w3, sh_w2=sh_w2)
    bp, sp_len, d = x_prompt.shape
    bs, ss_len, _ = x_sample.shape
    past = cache_mla_ckv.shape[2]
    depth = w_in.shape[0]
    assert d == D_MODEL and sp_len % CHUNK == 0 and ss_len == CHUNK
    tp, ts = bp * sp_len, bs * ss_len
    t = tp + ts

    cos2, sin2 = _rope_tables(bp, sp_len, bs, ss_len, past)
    x, xb = _ln_in(x_prompt.reshape(tp, d), x_sample.reshape(ts, d), ln_in_g, ln_in_b)

    new_p, new_s = [], []
    for l in range(depth):
        lp = _prep_layer(l, p)
        proj = _in_proj(xb, lp["w_in"])

        conv0 = jnp.concatenate([jnp.zeros((bp, CONV_WIDTH - 1, LRU_WIDTH), F32), state_lru_conv[l]], axis=0)
        h0 = jnp.concatenate([jnp.zeros((bp, LRU_WIDTH), F32), state_lru_h[l]], axis=0)[:, None, :]
        y_a, conv_new, h_new = _lru_branch(proj, conv0, h0, lru_conv_w[l], lru_conv_b[l], lp["wg"], lp["bg"],
                                           lp["sp"], bp=bp, sp_len=sp_len, bs=bs)
        h_new = h_new[:, 0, :]

        q = _q_proj(proj, mla_q_norm_g[l], lp["wuq"], cos2, sin2)
        ckv_new, kr_new, k, v = _kv_proj(proj, mla_kv_norm_g[l], lp["wuk"], lp["wuv"], cos2, sin2)
        kpast, vpast = _kv_cache_proj(cache_mla_ckv[l].reshape(bs * past, KV_LORA),
                                      cache_mla_krope[l].reshape(bs * past, QK_ROPE), lp["wuk"], lp["wuv"])
        yb_p = _attn_prompt(q, k, v, bp=bp, sp_len=sp_len)
        yb_s = _attn_sample(q, k, v, kpast, vpast, bs=bs, ss_len=ss_len, past=past, tp=tp)

        uc = proj[:, COL_UC:COL_UC + S5_WIDTH]
        zero_state = jnp.zeros((bp, S5_HID), F32)
        o_p, s5r_p, s5i_p = _s5_branch(_to_time_major(uc[:tp], bp, sp_len), zero_state, zero_state, lp["s5"],
                                       nb=bp, seq_len=sp_len)
        o_s, s5r_s, s5i_s = _s5_branch(_to_time_major(uc[tp:], bs, ss_len), state_s5_re[l].reshape(bs, S5_HID),
                                       state_s5_im[l].reshape(bs, S5_HID), lp["s5"], nb=bs, seq_len=ss_len)
        y_c = jnp.concatenate([_from_time_major(o_p, bp, sp_len), _from_time_major(o_s, bs, ss_len)], axis=0)

        merged = _merge(y_a, yb_p, yb_s, y_c, proj, lp["wa"], lp["wb"], lp["wc"])
        x, xb = _out_ln(x, merged, lp["w_out"], b_out[l], ln1_g[l], ln1_b[l])

        idx128, wt128 = _router(x, lp["router_w"], router_bias[l])
        blk_e, nvalid, buf_tok, buf_slot = _route_plan(idx128[:, :TOP_K], t)
        y6 = _experts(x, lp["w1"], lp["w3"], lp["w2"], blk_e, nvalid, buf_tok, buf_slot, n_out_rows=TOP_K * t)
        x, xb = _combine(x, xb, wt128, y6, lp["s1"], lp["s3"], lp["s2"], ln2_g[l], ln2_b[l])

        new_p.append((ckv_new[:tp].reshape(bp, sp_len, KV_LORA), kr_new[:tp].reshape(bp, sp_len, QK_ROPE),
                      conv_new[:bp], h_new[:bp], s5r_p.reshape(bp, S5_GROUPS, S5_STATE),
                      s5i_p.reshape(bp, S5_GROUPS, S5_STATE)))
        new_s.append((ckv_new[tp:].reshape(bs, ss_len, KV_LORA), kr_new[tp:].reshape(bs, ss_len, QK_ROPE),
                      conv_new[bp:], h_new[bp:], s5r_s.reshape(bs, S5_GROUPS, S5_STATE),
                      s5i_s.reshape(bs, S5_GROUPS, S5_STATE)))

    p_out = [jnp.stack(z) for z in zip(*new_p)]
    s_out = [jnp.stack(z) for z in zip(*new_s)]
    return (x[:tp].reshape(bp, sp_len, d), x[tp:].reshape(bs, ss_len, d), *p_out, *s_out)
```

```python
import functools
import math

import jax
import jax.numpy as jnp
import numpy as np
from jax import lax
from jax.experimental import pallas as pl
from jax.experimental.pallas import tpu as pltpu

F32 = jnp.float32
BF16 = jnp.bfloat16
I32 = jnp.int32

D_MODEL = 2048
DEPTH = 2
CHUNK = 64
LN_EPS = 1e-5
RMS_EPS = 1e-6
ALPHA = (2 * DEPTH) ** 0.25
LRU_WIDTH = 1024
LRU_BLOCKS = 8
LRU_BLOCK = 128
CONV_WIDTH = 4
LRU_C = 8.0
MLA_HEADS = 16
QK_NOPE = 128
QK_ROPE = 64
QK_DIM = QK_NOPE + QK_ROPE
V_HEAD = 128
Q_LORA = 512
KV_LORA = 512
ROPE_THETA = 10000.0
S5_WIDTH = 1024
S5_GROUP = 16
S5_GROUPS = 64
S5_STATE = 64
S5_HID = S5_GROUPS * S5_STATE
N_EXPERTS = 64
TOP_K = 6
N_GROUPS = 8
GROUP_SIZE = N_EXPERTS // N_GROUPS
TOPK_GROUPS = 4
EXPERT_FF = 512
ROUTE_SCALE = 2.5

COL_XA, COL_GA, COL_CQ, COL_CKV, COL_UC, COL_GATES, COL_KR = 0, 1024, 2048, 2560, 3072, 4096, 10240
N_PROJ = 10368

V7X_VMEM_BYTES = 64 * 1024 * 1024
VMEM_LIMIT = V7X_VMEM_BYTES - 8 * 1024 * 1024
LANES = 128
SUBLANES = 8

EXPERT_ROWS = 256
ATTN_TQ = 512


def _cparams(sem):
    return pltpu.CompilerParams(dimension_semantics=sem, vmem_limit_bytes=VMEM_LIMIT)


def _row_tile(n, pref):
    for t in (512, 256, 128, 64, 32, 16, 8):
        if t <= pref and n % t == 0:
            return t
    raise ValueError(f"no row tile for {n}")


def _ln_rows(x, g, b):
    mu = jnp.mean(x, axis=-1, keepdims=True)
    xc = x - mu
    var = jnp.mean(xc * xc, axis=-1, keepdims=True)
    return xc * lax.rsqrt(var + LN_EPS) * g + b


def _dot(a, b):
    return jnp.dot(a, b, preferred_element_type=F32)


def _ln_in_kernel(xp_ref, xs_ref, g_ref, b_ref, o_ref, ob_ref, *, n_prompt_tiles):
    i = pl.program_id(0)

    def emit(x):
        y = _ln_rows(x, g_ref[...], b_ref[...])
        o_ref[...] = y
        ob_ref[...] = y.astype(BF16)

    @pl.when(i < n_prompt_tiles)
    def _():
        emit(xp_ref[...])

    @pl.when(i >= n_prompt_tiles)
    def _():
        emit(xs_ref[...])


def _ln_in(xp, xs, g, b):
    tp, ts = xp.shape[0], xs.shape[0]
    tm = _row_tile(math.gcd(tp, ts), 512)
    npt, nst = tp // tm, ts // tm
    t = tp + ts
    return pl.pallas_call(
        functools.partial(_ln_in_kernel, n_prompt_tiles=npt),
        grid=(npt + nst,),
        in_specs=[
            pl.BlockSpec((tm, D_MODEL), lambda i: (jnp.minimum(i, npt - 1), 0)),
            pl.BlockSpec((tm, D_MODEL), lambda i: (jnp.maximum(i - npt, 0), 0)),
            pl.BlockSpec((1, D_MODEL), lambda i: (0, 0)),
            pl.BlockSpec((1, D_MODEL), lambda i: (0, 0)),
        ],
        out_specs=[pl.BlockSpec((tm, D_MODEL), lambda i: (i, 0)),
                   pl.BlockSpec((tm, D_MODEL), lambda i: (i, 0))],
        out_shape=[jax.ShapeDtypeStruct((t, D_MODEL), F32), jax.ShapeDtypeStruct((t, D_MODEL), BF16)],
        compiler_params=_cparams(("parallel",)),
        name="ln_in",
    )(xp, xs, g.reshape(1, -1), b.reshape(1, -1))


def _mm_kernel(x_ref, w_ref, o_ref):
    o_ref[...] = _dot(x_ref[...], w_ref[...])


def _in_proj(xb, w):
    t, k = xb.shape
    n = w.shape[1]
    tm = _row_tile(t, 512)
    tn = 1152
    assert n % tn == 0
    return pl.pallas_call(
        _mm_kernel,
        grid=(n // tn, t // tm),
        in_specs=[pl.BlockSpec((tm, k), lambda j, i: (i, 0)),
                  pl.BlockSpec((k, tn), lambda j, i: (0, j))],
        out_specs=pl.BlockSpec((tm, tn), lambda j, i: (i, j)),
        out_shape=jax.ShapeDtypeStruct((t, n), F32),
        compiler_params=_cparams(("parallel", "parallel")),
        name="in_proj",
    )(xb, w)


def _lru_kernel(xa_ref, ga_ref, cs_ref, h0_ref, cw_ref, cb_ref, wg_ref, bg_ref, sp_ref,
                y_ref, cn_ref, hn_ref, xbuf, hcar, *, rows, n_prompt_tiles, tiles_per_seq):
    c = pl.program_id(0)
    first = jnp.logical_or(c >= n_prompt_tiles, c % tiles_per_seq == 0)

    @pl.when(first)
    def _():
        xbuf[5:8, :] = cs_ref[0]
        hcar[0:1, :] = h0_ref[0]

    xa = xa_ref[...]
    xbuf[8:8 + rows, :] = xa
    cw = cw_ref[...]
    xc = (cb_ref[...] + cw[0:1] * xbuf[5:5 + rows, :] + cw[1:2] * xbuf[6:6 + rows, :]
          + cw[2:3] * xbuf[7:7 + rows, :] + cw[3:4] * xa)
    tail = xa[rows - 3:rows, :]
    xbuf[5:8, :] = tail
    cn_ref[0] = tail

    row = lax.broadcasted_iota(I32, (rows, LRU_BLOCK), 0)
    for n in range(LRU_BLOCKS):
        sl = slice(n * LRU_BLOCK, (n + 1) * LRU_BLOCK)
        xcb = xc[:, sl]
        g = _dot(xcb.astype(BF16), wg_ref[n]) + bg_ref[n]
        r = jax.nn.sigmoid(g[:, :LRU_BLOCK])
        gi = jax.nn.sigmoid(g[:, LRU_BLOCK:])
        log_a = -LRU_C * r * sp_ref[:, sl]
        a = jnp.exp(log_a)
        th = jnp.tanh(log_a)
        b = jnp.sqrt(-2.0 * th / (1.0 - th)) * (gi * xcb)
        s = 1
        while s < rows:
            keep = row >= s
            a_s = jnp.where(keep, pltpu.roll(a, s, 0), 1.0)
            b_s = jnp.where(keep, pltpu.roll(b, s, 0), 0.0)
            b = a * b_s + b
            a = a * a_s
            s *= 2
        h = a * hcar[0:1, sl] + b
        hcar[0:1, sl] = h[rows - 1:rows, :]
        y_ref[:, sl] = (h * jax.nn.gelu(ga_ref[:, sl])).astype(BF16)
    hn_ref[0] = hcar[0:1, :]


def _lru_branch(proj, conv_state, h_state, conv_w, conv_b, wg, bg, sp, *, bp, sp_len, bs):
    t = proj.shape[0]
    rows = CHUNK
    tps = sp_len // rows
    npt = bp * tps
    nt = t // rows
    nseq = bp + bs
    w = LRU_WIDTH

    def seq_of(c):
        return jnp.where(c < npt, c // tps, bp + (c - npt))

    return pl.pallas_call(
        functools.partial(_lru_kernel, rows=rows, n_prompt_tiles=npt, tiles_per_seq=tps),
        grid=(nt,),
        in_specs=[
            pl.BlockSpec((rows, w), lambda c: (c, COL_XA // w)),
            pl.BlockSpec((rows, w), lambda c: (c, COL_GA // w)),
            pl.BlockSpec((1, 3, w), lambda c: (seq_of(c), 0, 0)),
            pl.BlockSpec((1, 1, w), lambda c: (seq_of(c), 0, 0)),
            pl.BlockSpec((CONV_WIDTH, w), lambda c: (0, 0)),
            pl.BlockSpec((1, w), lambda c: (0, 0)),
            pl.BlockSpec((LRU_BLOCKS, LRU_BLOCK, 2 * LRU_BLOCK), lambda c: (0, 0, 0)),
            pl.BlockSpec((LRU_BLOCKS, 1, 2 * LRU_BLOCK), lambda c: (0, 0, 0)),
            pl.BlockSpec((1, w), lambda c: (0, 0)),
        ],
        out_specs=[
            pl.BlockSpec((rows, w), lambda c: (c, 0)),
            pl.BlockSpec((1, 3, w), lambda c: (seq_of(c), 0, 0)),
            pl.BlockSpec((1, 1, w), lambda c: (seq_of(c), 0, 0)),
        ],
        out_shape=[jax.ShapeDtypeStruct((t, w), BF16),
                   jax.ShapeDtypeStruct((nseq, 3, w), F32),
                   jax.ShapeDtypeStruct((nseq, 1, w), F32)],
        scratch_shapes=[pltpu.VMEM((rows + 8, w), F32), pltpu.VMEM((SUBLANES, w), F32)],
        compiler_params=_cparams(("arbitrary",)),
        name="rglru",
    )(proj, proj, conv_state, h_state, conv_w, conv_b.reshape(1, -1), wg, bg, sp.reshape(1, -1))


S5_COLS = 512
S5_RG = 4


def _s5_kernel(u_ref, h0r_ref, h0i_ref, bbr_ref, bbi_ref, ccr_ref, cci_ref, ar_ref, ai_ref, d_ref,
               gw_ref, gb_ref, o_ref, hr_out, hi_out, bur, bui, hr, hi, *, nb, steps):
    i = pl.program_id(0)

    @pl.when(i == 0)
    def _():
        hr[...] = h0r_ref[...]
        hi[...] = h0i_ref[...]

    u = u_ref[...]
    ub = u.astype(BF16)
    kin = S5_WIDTH // S5_RG
    kst = S5_HID // S5_RG
    for r in range(S5_RG):
        ur = ub[:, r * kin:(r + 1) * kin]
        bur[:, r * kst:(r + 1) * kst] = _dot(ur, bbr_ref[r])
        bui[:, r * kst:(r + 1) * kst] = _dot(ur, bbi_ref[r])

    for cc in range(S5_HID // S5_COLS):
        cs = slice(cc * S5_COLS, (cc + 1) * S5_COLS)
        ar = jnp.broadcast_to(ar_ref[:, cs], (SUBLANES, S5_COLS))
        ai = jnp.broadcast_to(ai_ref[:, cs], (SUBLANES, S5_COLS))
        for bg in range(nb // SUBLANES):
            bsl = slice(bg * SUBLANES, (bg + 1) * SUBLANES)

            def body(t, carry, cs=cs, bg=bg, ar=ar, ai=ai):
                h_r, h_i = carry
                rsl = pl.ds(pl.multiple_of(t * nb + bg * SUBLANES, SUBLANES), SUBLANES)
                n_r = ar * h_r - ai * h_i + bur[rsl, cs]
                n_i = ar * h_i + ai * h_r + bui[rsl, cs]
                bur[rsl, cs] = n_r
                bui[rsl, cs] = n_i
                return n_r, n_i

            h_r, h_i = lax.fori_loop(0, steps, body, (hr[bsl, cs], hi[bsl, cs]))
            hr[bsl, cs] = h_r
            hi[bsl, cs] = h_i

    ys = []
    for r in range(S5_RG):
        hrb = bur[:, r * kst:(r + 1) * kst].astype(BF16)
        hib = bui[:, r * kst:(r + 1) * kst].astype(BF16)
        ys.append(_dot(hrb, ccr_ref[r]) - _dot(hib, cci_ref[r]))
    y = jnp.concatenate(ys, axis=-1) + d_ref[...] * u
    z = jax.nn.gelu(y)
    gate = jax.nn.sigmoid(_dot(z.astype(BF16), gw_ref[...]) + gb_ref[...])
    o_ref[...] = (z * gate).astype(BF16)
    hr_out[...] = hr[...]
    hi_out[...] = hi[...]


def _s5_branch(u_tm, h0r, h0i, pw, *, nb, seq_len):
    assert nb % SUBLANES == 0
    steps = max(1, min(seq_len, 256 // nb))
    while seq_len % steps:
        steps -= 1
    rows = steps * nb
    kin = S5_WIDTH // S5_RG
    kst = S5_HID // S5_RG
    const2 = lambda i: (0, 0)
    const3 = lambda i: (0, 0, 0)
    return pl.pallas_call(
        functools.partial(_s5_kernel, nb=nb, steps=steps),
        grid=(seq_len // steps,),
        in_specs=[
            pl.BlockSpec((rows, S5_WIDTH), lambda i: (i, 0)),
            pl.BlockSpec((nb, S5_HID), const2),
            pl.BlockSpec((nb, S5_HID), const2),
            pl.BlockSpec((S5_RG, kin, kst), const3),
            pl.BlockSpec((S5_RG, kin, kst), const3),
            pl.BlockSpec((S5_RG, kst, kin), const3),
            pl.BlockSpec((S5_RG, kst, kin), const3),
            pl.BlockSpec((1, S5_HID), const2),
            pl.BlockSpec((1, S5_HID), const2),
            pl.BlockSpec((1, S5_WIDTH), const2),
            pl.BlockSpec((S5_WIDTH, S5_WIDTH), const2),
            pl.BlockSpec((1, S5_WIDTH), const2),
        ],
        out_specs=[pl.BlockSpec((rows, S5_WIDTH), lambda i: (i, 0)),
                   pl.BlockSpec((nb, S5_HID), const2),
                   pl.BlockSpec((nb, S5_HID), const2)],
        out_shape=[jax.ShapeDtypeStruct((seq_len * nb, S5_WIDTH), BF16),
                   jax.ShapeDtypeStruct((nb, S5_HID), F32),
                   jax.ShapeDtypeStruct((nb, S5_HID), F32)],
        scratch_shapes=[pltpu.VMEM((rows, S5_HID), F32), pltpu.VMEM((rows, S5_HID), F32),
                        pltpu.VMEM((nb, S5_HID), F32), pltpu.VMEM((nb, S5_HID), F32)],
        compiler_params=_cparams(("arbitrary",)),
        name="s5",
    )(u_tm, h0r, h0i, pw["bbr"], pw["bbi"], pw["ccr"], pw["cci"], pw["ar"], pw["ai"], pw["d"],
      pw["glu_w"], pw["glu_b"])


def _rms_rows(x, g):
    return x * lax.rsqrt(jnp.mean(x * x, axis=-1, keepdims=True) + RMS_EPS) * g


def _qproj_kernel(cq_ref, g_ref, w_ref, cos_ref, sin_ref, q_ref):
    xn = _rms_rows(cq_ref[...], g_ref[...]).astype(BF16)
    cos2 = cos_ref[...]
    sin2 = sin_ref[...]
    for h in range(MLA_HEADS):
        r = _dot(xn, w_ref[h])
        q_ref[h, :, 0:QK_NOPE] = r[:, 0:QK_NOPE].astype(BF16)
        rot = r[:, QK_NOPE:QK_DIM] * cos2 + r[:, QK_DIM:QK_DIM + QK_ROPE] * sin2
        q_ref[h, :, QK_NOPE:QK_DIM] = rot.astype(BF16)


def _q_proj(proj, g, w_aug, cos2, sin2):
    t = proj.shape[0]
    tm = _row_tile(t, 256)
    return pl.pallas_call(
        _qproj_kernel,
        grid=(t // tm,),
        in_specs=[
            pl.BlockSpec((tm, Q_LORA), lambda i: (i, COL_CQ // Q_LORA)),
            pl.BlockSpec((1, Q_LORA), lambda i: (0, 0)),
            pl.BlockSpec((MLA_HEADS, Q_LORA, 2 * LANES), lambda i: (0, 0, 0)),
            pl.BlockSpec((tm, QK_ROPE), lambda i: (i, 0)),
            pl.BlockSpec((tm, QK_ROPE), lambda i: (i, 0)),
        ],
        out_specs=pl.BlockSpec((MLA_HEADS, tm, QK_DIM), lambda i: (0, i, 0)),
        out_shape=jax.ShapeDtypeStruct((MLA_HEADS, t, QK_DIM), BF16),
        compiler_params=_cparams(("parallel",)),
        name="mla_q_proj",
    )(proj, g.reshape(1, -1), w_aug, cos2, sin2)


def _kvproj_kernel(ckv_ref, kr_ref, g_ref, wk_ref, wv_ref, cos_ref, sin_ref,
                   ckvn_ref, krn_ref, k_ref, v_ref):
    xn = _rms_rows(ckv_ref[...], g_ref[...])
    ckvn_ref[...] = xn
    xb = xn.astype(BF16)
    kr = kr_ref[...]
    rot = kr[:, 0:QK_ROPE] * cos_ref[...] + kr[:, QK_ROPE:2 * QK_ROPE] * sin_ref[...]
    krn_ref[...] = rot
    rb = rot.astype(BF16)
    for h in range(MLA_HEADS):
        k_ref[h, :, 0:QK_NOPE] = _dot(xb, wk_ref[h]).astype(BF16)
        k_ref[h, :, QK_NOPE:QK_DIM] = rb
        v_ref[h] = _dot(xb, wv_ref[h]).astype(BF16)


def _kv_proj(proj, g, wk, wv, cos2, sin2):
    t = proj.shape[0]
    tm = _row_tile(t, 256)
    c3 = lambda i: (0, 0, 0)
    return pl.pallas_call(
        _kvproj_kernel,
        grid=(t // tm,),
        in_specs=[
            pl.BlockSpec((tm, KV_LORA), lambda i: (i, COL_CKV // KV_LORA)),
            pl.BlockSpec((tm, LANES), lambda i: (i, COL_KR // LANES)),
            pl.BlockSpec((1, KV_LORA), lambda i: (0, 0)),
            pl.BlockSpec((MLA_HEADS, KV_LORA, QK_NOPE), c3),
            pl.BlockSpec((MLA_HEADS, KV_LORA, V_HEAD), c3),
            pl.BlockSpec((tm, QK_ROPE), lambda i: (i, 0)),
            pl.BlockSpec((tm, QK_ROPE), lambda i: (i, 0)),
        ],
        out_specs=[
            pl.BlockSpec((tm, KV_LORA), lambda i: (i, 0)),
            pl.BlockSpec((tm, QK_ROPE), lambda i: (i, 0)),
            pl.BlockSpec((MLA_HEADS, tm, QK_DIM), lambda i: (0, i, 0)),
            pl.BlockSpec((MLA_HEADS, tm, V_HEAD), lambda i: (0, i, 0)),
        ],
        out_shape=[jax.ShapeDtypeStruct((t, KV_LORA), F32),
                   jax.ShapeDtypeStruct((t, QK_ROPE), F32),
                   jax.ShapeDtypeStruct((MLA_HEADS, t, QK_DIM), BF16),
                   jax.ShapeDtypeStruct((MLA_HEADS, t, V_HEAD), BF16)],
        compiler_params=_cparams(("parallel",)),
        name="mla_kv_proj",
    )(proj, proj, g.reshape(1, -1), wk, wv, cos2, sin2)


def _kvcache_kernel(ckv_ref, kr_ref, wk_ref, wv_ref, k_ref, v_ref):
    xb = ckv_ref[...].astype(BF16)
    rb = kr_ref[...].astype(BF16)
    for h in range(MLA_HEADS):
        k_ref[h, :, 0:QK_NOPE] = _dot(xb, wk_ref[h]).astype(BF16)
        k_ref[h, :, QK_NOPE:QK_DIM] = rb
        v_ref[h] = _dot(xb, wv_ref[h]).astype(BF16)


def _kv_cache_proj(ckv, kr, wk, wv):
    t = ckv.shape[0]
    tm = _row_tile(t, 256)
    c3 = lambda i: (0, 0, 0)
    return pl.pallas_call(
        _kvcache_kernel,
        grid=(t // tm,),
        in_specs=[
            pl.BlockSpec((tm, KV_LORA), lambda i: (i, 0)),
            pl.BlockSpec((tm, QK_ROPE), lambda i: (i, 0)),
            pl.BlockSpec((MLA_HEADS, KV_LORA, QK_NOPE), c3),
            pl.BlockSpec((MLA_HEADS, KV_LORA, V_HEAD), c3),
        ],
        out_specs=[pl.BlockSpec((MLA_HEADS, tm, QK_DIM), lambda i: (0, i, 0)),
                   pl.BlockSpec((MLA_HEADS, tm, V_HEAD), lambda i: (0, i, 0))],
        out_shape=[jax.ShapeDtypeStruct((MLA_HEADS, t, QK_DIM), BF16),
                   jax.ShapeDtypeStruct((MLA_HEADS, t, V_HEAD), BF16)],
        compiler_params=_cparams(("parallel",)),
        name="mla_kv_cache_proj",
    )(ckv, kr, wk, wv)


def _qk(q, k):
    return lax.dot_general(q, k, (((1,), (1,)), ((), ())), preferred_element_type=F32)


def _attn_prompt_kernel(q_ref, k_ref, v_ref, o_ref, *, tq, scale):
    qi = pl.program_id(2)
    c = scale * math.log2(math.e)
    q = q_ref[0]

    def kv_rows(ki):
        return pl.ds(pl.multiple_of(ki * tq, tq), tq)

    def scores(ki):
        return _qk(q, k_ref[0, kv_rows(ki), :])

    def update(state, s, ki):
        m, l, acc = state
        m_new = jnp.maximum(m, jnp.max(s, axis=-1, keepdims=True))
        alpha = jnp.exp2((m - m_new) * c)
        p = jnp.exp2((s - m_new) * c)
        l = alpha * l + jnp.sum(p, axis=-1, keepdims=True)
        acc = alpha * acc + _dot(p.astype(BF16), v_ref[0, kv_rows(ki), :])
        return m_new, l, acc

    init = (jnp.full((tq, 1), -jnp.inf, F32), jnp.zeros((tq, 1), F32), jnp.zeros((tq, V_HEAD), F32))
    state = lax.fori_loop(0, qi, lambda ki, st: update(st, scores(ki), ki), init)
    rr = lax.broadcasted_iota(I32, (tq, tq), 0) // CHUNK
    cc = lax.broadcasted_iota(I32, (tq, tq), 1) // CHUNK
    _, l, acc = update(state, jnp.where(cc <= rr, scores(qi), -jnp.inf), qi)
    o_ref[...] = (acc / l).astype(BF16)


def _attn_prompt(q, k, v, *, bp, sp_len):
    tq = _row_tile(sp_len, ATTN_TQ)
    nq = sp_len // tq
    scale = QK_DIM ** -0.5
    return pl.pallas_call(
        functools.partial(_attn_prompt_kernel, tq=tq, scale=scale),
        grid=(bp, MLA_HEADS, nq),
        in_specs=[
            pl.BlockSpec((1, tq, QK_DIM), lambda b, h, i: (h, b * nq + i, 0)),
            pl.BlockSpec((1, sp_len, QK_DIM), lambda b, h, i: (h, b, 0)),
            pl.BlockSpec((1, sp_len, V_HEAD), lambda b, h, i: (h, b, 0)),
        ],
        out_specs=pl.BlockSpec((tq, V_HEAD), lambda b, h, i: (b * nq + i, h)),
        out_shape=jax.ShapeDtypeStruct((bp * sp_len, MLA_HEADS * V_HEAD), BF16),
        compiler_params=_cparams(("parallel", "parallel", "arbitrary")),
        name="mla_attn_prompt",
    )(q, k, v)


def _attn_sample_kernel(q_ref, kp_ref, vp_ref, kn_ref, vn_ref, o_ref, *, scale):
    for h in range(MLA_HEADS):
        q = q_ref[h]
        s1 = _qk(q, kp_ref[h]) * scale
        s2 = _qk(q, kn_ref[h]) * scale
        m = jnp.maximum(jnp.max(s1, axis=-1, keepdims=True), jnp.max(s2, axis=-1, keepdims=True))
        p1 = jnp.exp(s1 - m)
        p2 = jnp.exp(s2 - m)
        l = jnp.sum(p1, axis=-1, keepdims=True) + jnp.sum(p2, axis=-1, keepdims=True)
        acc = _dot(p1.astype(BF16), vp_ref[h]) + _dot(p2.astype(BF16), vn_ref[h])
        o_ref[:, h * V_HEAD:(h + 1) * V_HEAD] = (acc / l).astype(BF16)


def _attn_sample(q, k, v, kpast, vpast, *, bs, ss_len, past, tp):
    assert ss_len == CHUNK and past % CHUNK == 0 and tp % ss_len == 0
    off = tp // ss_len
    scale = QK_DIM ** -0.5
    hh = MLA_HEADS
    return pl.pallas_call(
        functools.partial(_attn_sample_kernel, scale=scale),
        grid=(bs,),
        in_specs=[
            pl.BlockSpec((hh, ss_len, QK_DIM), lambda b: (0, off + b, 0)),
            pl.BlockSpec((hh, past, QK_DIM), lambda b: (0, b, 0)),
            pl.BlockSpec((hh, past, V_HEAD), lambda b: (0, b, 0)),
            pl.BlockSpec((hh, ss_len, QK_DIM), lambda b: (0, off + b, 0)),
            pl.BlockSpec((hh, ss_len, V_HEAD), lambda b: (0, off + b, 0)),
        ],
        out_specs=pl.BlockSpec((ss_len, hh * V_HEAD), lambda b: (b, 0)),
        out_shape=jax.ShapeDtypeStruct((bs * ss_len, hh * V_HEAD), BF16),
        compiler_params=_cparams(("parallel",)),
        name="mla_attn_sample",
    )(q, kpast, vpast, k, v)


def _merge_kernel(ya_ref, ybp_ref, ybs_ref, yc_ref, ga_ref, gb_ref, gc_ref, wa_ref, wb_ref, wc_ref, o_ref, *,
                  n_prompt_tiles):
    i = pl.program_id(1)

    def emit(yb):
        m = jax.nn.sigmoid(ga_ref[...]) * _dot(ya_ref[...], wa_ref[...])
        m = m + jax.nn.sigmoid(gb_ref[...]) * _dot(yb, wb_ref[...])
        m = m + jax.nn.sigmoid(gc_ref[...]) * _dot(yc_ref[...], wc_ref[...])
        o_ref[...] = m.astype(BF16)

    @pl.when(i < n_prompt_tiles)
    def _():
        emit(ybp_ref[...])

    @pl.when(i >= n_prompt_tiles)
    def _():
        emit(ybs_ref[...])


def _merge(ya, yb_p, yb_s, yc, proj, wa, wb, wc):
    t = ya.shape[0]
    tp, ts = yb_p.shape[0], yb_s.shape[0]
    tm = _row_tile(math.gcd(tp, ts), 256)
    npt = tp // tm
    tn = 1024
    nn = D_MODEL // tn
    g0 = COL_GATES // tn
    return pl.pallas_call(
        functools.partial(_merge_kernel, n_prompt_tiles=npt),
        grid=(nn, t // tm),
        in_specs=[
            pl.BlockSpec((tm, LRU_WIDTH), lambda j, i: (i, 0)),
            pl.BlockSpec((tm, MLA_HEADS * V_HEAD), lambda j, i: (jnp.minimum(i, npt - 1), 0)),
            pl.BlockSpec((tm, MLA_HEADS * V_HEAD), lambda j, i: (jnp.maximum(i - npt, 0), 0)),
            pl.BlockSpec((tm, S5_WIDTH), lambda j, i: (i, 0)),
            pl.BlockSpec((tm, tn), lambda j, i: (i, g0 + j)),
            pl.BlockSpec((tm, tn), lambda j, i: (i, g0 + nn + j)),
            pl.BlockSpec((tm, tn), lambda j, i: (i, g0 + 2 * nn + j)),
            pl.BlockSpec((LRU_WIDTH, tn), lambda j, i: (0, j)),
            pl.BlockSpec((MLA_HEADS * V_HEAD, tn), lambda j, i: (0, j)),
            pl.BlockSpec((S5_WIDTH, tn), lambda j, i: (0, j)),
        ],
        out_specs=pl.BlockSpec((tm, tn), lambda j, i: (i, j)),
        out_shape=jax.ShapeDtypeStruct((t, D_MODEL), BF16),
        compiler_params=_cparams(("parallel", "parallel")),
        name="branch_merge",
    )(ya, yb_p, yb_s, yc, proj, proj, proj, wa, wb, wc)


def _outln_kernel(x_ref, m_ref, w_ref, bo_ref, g_ref, b_ref, o_ref, ob_ref):
    y = ALPHA * x_ref[...] + _dot(m_ref[...], w_ref[...]) + bo_ref[...]
    y = _ln_rows(y, g_ref[...], b_ref[...])
    o_ref[...] = y
    ob_ref[...] = y.astype(BF16)


def _out_ln(x, merged, w_out, b_out, g, b):
    t = x.shape[0]
    tm = _row_tile(t, 256)
    c2 = lambda i: (0, 0)
    return pl.pallas_call(
        _outln_kernel,
        grid=(t // tm,),
        in_specs=[
            pl.BlockSpec((tm, D_MODEL), lambda i: (i, 0)),
            pl.BlockSpec((tm, D_MODEL), lambda i: (i, 0)),
            pl.BlockSpec((D_MODEL, D_MODEL), c2),
            pl.BlockSpec((1, D_MODEL), c2),
            pl.BlockSpec((1, D_MODEL), c2),
            pl.BlockSpec((1, D_MODEL), c2),
        ],
        out_specs=[pl.BlockSpec((tm, D_MODEL), lambda i: (i, 0)),
                   pl.BlockSpec((tm, D_MODEL), lambda i: (i, 0))],
        out_shape=[jax.ShapeDtypeStruct((t, D_MODEL), F32), jax.ShapeDtypeStruct((t, D_MODEL), BF16)],
        compiler_params=_cparams(("parallel",)),
        name="out_proj_ln1",
    )(x, merged, w_out, b_out.reshape(1, -1), g.reshape(1, -1), b.reshape(1, -1))


def _router_kernel(x_ref, w_ref, bias_ref, idx_ref, wt_ref):
    tm = x_ref.shape[0]
    scores = jax.nn.sigmoid(_dot(x_ref[...], w_ref[...]))
    choice = scores + bias_ref[...]
    lane = lax.broadcasted_iota(I32, (tm, N_EXPERTS), 1)
    grp = lane // GROUP_SIZE
    neg = -jnp.inf

    def first_argmax(v):
        mx = jnp.max(v, axis=-1, keepdims=True)
        ix = jnp.min(jnp.where(v == mx, lane, N_EXPERTS), axis=-1, keepdims=True)
        return mx, ix

    gscore = []
    for g in range(N_GROUPS):
        v = jnp.where(grp == g, choice, neg)
        m1, i1 = first_argmax(v)
        m2 = jnp.max(jnp.where(lane == i1, neg, v), axis=-1, keepdims=True)
        gscore.append(m1 + m2)
    taken = [jnp.zeros((tm, 1), jnp.bool_) for _ in range(N_GROUPS)]
    for _ in range(TOPK_GROUPS):
        best = None
        for g in range(N_GROUPS):
            cand = jnp.where(taken[g], neg, gscore[g])
            best = cand if best is None else jnp.maximum(best, cand)
        found = jnp.zeros((tm, 1), jnp.bool_)
        for g in range(N_GROUPS):
            pick = jnp.logical_and(jnp.logical_and(gscore[g] == best, ~taken[g]), ~found)
            found = jnp.logical_or(found, pick)
            taken[g] = jnp.logical_or(taken[g], pick)
    emask = jnp.zeros((tm, N_EXPERTS), jnp.bool_)
    for g in range(N_GROUPS):
        emask = jnp.logical_or(emask, jnp.logical_and(grp == g, taken[g]))
    masked = jnp.where(emask, choice, neg)
    out_lane = lax.broadcasted_iota(I32, (tm, LANES), 1)
    idx_out = jnp.zeros((tm, LANES), I32)
    wt_out = jnp.zeros((tm, LANES), F32)
    wsum = jnp.zeros((tm, 1), F32)
    for k in range(TOP_K):
        _, ik = first_argmax(masked)
        hit = lane == ik
        wk = jnp.sum(jnp.where(hit, scores, 0.0), axis=-1, keepdims=True)
        masked = jnp.where(hit, neg, masked)
        wsum = wsum + wk
        idx_out = jnp.where(out_lane == k, ik, idx_out)
        wt_out = jnp.where(out_lane == k, wk, wt_out)
    idx_ref[...] = idx_out
    wt_ref[...] = wt_out / wsum * ROUTE_SCALE


def _router(xb, w, bias):
    t = xb.shape[0]
    tm = _row_tile(t, 256)
    return pl.pallas_call(
        _router_kernel,
        grid=(t // tm,),
        in_specs=[pl.BlockSpec((tm, D_MODEL), lambda i: (i, 0)),
                  pl.BlockSpec((D_MODEL, N_EXPERTS), lambda i: (0, 0)),
                  pl.BlockSpec((1, N_EXPERTS), lambda i: (0, 0))],
        out_specs=[pl.BlockSpec((tm, LANES), lambda i: (i, 0)),
                   pl.BlockSpec((tm, LANES), lambda i: (i, 0))],
        out_shape=[jax.ShapeDtypeStruct((t, LANES), I32), jax.ShapeDtypeStruct((t, LANES), F32)],
        compiler_params=_cparams(("parallel",)),
        name="moe_router",
    )(xb, w, bias.reshape(1, -1))


def _expert_kernel(blk_e_ref, nvalid_ref, tok_ref, tokn_ref, slot_ref, x_hbm, w1_ref, w3_ref, w2_ref,
                   o_hbm, xbuf, ybuf, w1b, w3b, w2b, gsem, ssem, *, rows):
    b = pl.program_id(0)
    nb = pl.num_programs(0)
    cur = b % 2
    nxt = 1 - cur

    def start_gather(tref, buf):
        for j in range(rows):
            pltpu.make_async_copy(x_hbm.at[pl.ds(tref[0, 0, j], 1), :], xbuf.at[buf, pl.ds(j, 1), :],
                                  gsem.at[buf]).start(priority=j % 2)

    def wait_gather(buf):
        pltpu.make_async_copy(x_hbm.at[pl.ds(0, rows), :], xbuf.at[buf], gsem.at[buf]).wait()

    def scatter_row(j, slot, buf):
        return pltpu.make_async_copy(ybuf.at[buf, pl.ds(j, 1), :], o_hbm.at[pl.ds(slot, 1), :], ssem.at[buf])

    def start_scatter(count, buf):
        def body(j, carry):
            scatter_row(j, slot_ref[0, 0, j], buf).start()
            return carry

        @pl.when(count == rows)
        def _():
            for j in range(rows):
                scatter_row(j, slot_ref[0, 0, j], buf).start(priority=j % 2)

        @pl.when(count < rows)
        def _():
            lax.fori_loop(0, count, body, 0)

    def wait_scatter(count, buf):
        @pl.when(count == rows)
        def _():
            pltpu.make_async_copy(ybuf.at[buf], o_hbm.at[pl.ds(0, rows), :], ssem.at[buf]).wait()

        @pl.when(count < rows)
        def _():
            def body(j, carry):
                scatter_row(0, 0, buf).wait()
                return carry
            lax.fori_loop(0, count, body, 0)

    @pl.when(b == 0)
    def _():
        start_gather(tok_ref, cur)

    @pl.when(b + 1 < nb)
    def _():
        start_gather(tokn_ref, nxt)

    wait_gather(cur)

    @pl.when(b >= 2)
    def _():
        wait_scatter(nvalid_ref[jnp.maximum(b - 2, 0)], cur)

    @pl.when(jnp.logical_or(b == 0, blk_e_ref[b] != blk_e_ref[jnp.maximum(b - 1, 0)]))
    def _():
        w1b[...] = w1_ref[0].astype(BF16)
        w3b[...] = w3_ref[0].astype(BF16)
        w2b[...] = w2_ref[0].astype(BF16)

    nv = nvalid_ref[b]

    @pl.when(nv > 0)
    def _():
        x = xbuf[cur].astype(BF16)
        h = jax.nn.silu(_dot(x, w1b[...])) * _dot(x, w3b[...])
        ybuf[cur] = _dot(h.astype(BF16), w2b[...])
        start_scatter(nv, cur)

    @pl.when(b == nb - 1)
    def _():
        wait_scatter(nv, cur)

        @pl.when(b >= 1)
        def _():
            wait_scatter(nvalid_ref[jnp.maximum(b - 1, 0)], nxt)


def _experts(x, w1, w3, w2, blk_e, nvalid, buf_tok, buf_slot, *, n_out_rows):
    rows = EXPERT_ROWS
    n_blocks = blk_e.shape[0]
    tok3 = buf_tok.reshape(n_blocks, 1, rows)
    slot3 = buf_slot.reshape(n_blocks, 1, rows)
    smem_blk = lambda f: pl.BlockSpec((1, 1, rows), f, memory_space=pltpu.SMEM)
    grid_spec = pltpu.PrefetchScalarGridSpec(
        num_scalar_prefetch=2,
        grid=(n_blocks,),
        in_specs=[
            smem_blk(lambda b, be, nv: (b, 0, 0)),
            smem_blk(lambda b, be, nv: (jnp.minimum(b + 1, n_blocks - 1), 0, 0)),
            smem_blk(lambda b, be, nv: (b, 0, 0)),
            pl.BlockSpec(memory_space=pl.ANY),
            pl.BlockSpec((1, D_MODEL, EXPERT_FF), lambda b, be, nv: (be[b], 0, 0)),
            pl.BlockSpec((1, D_MODEL, EXPERT_FF), lambda b, be, nv: (be[b], 0, 0)),
            pl.BlockSpec((1, EXPERT_FF, D_MODEL), lambda b, be, nv: (be[b], 0, 0)),
        ],
        out_specs=pl.BlockSpec(memory_space=pl.ANY),
        scratch_shapes=[pltpu.VMEM((2, rows, D_MODEL), F32), pltpu.VMEM((2, rows, D_MODEL), F32),
                        pltpu.VMEM((D_MODEL, EXPERT_FF), BF16), pltpu.VMEM((D_MODEL, EXPERT_FF), BF16),
                        pltpu.VMEM((EXPERT_FF, D_MODEL), BF16),
                        pltpu.SemaphoreType.DMA((2,)), pltpu.SemaphoreType.DMA((2,))],
    )
    return pl.pallas_call(
        functools.partial(_expert_kernel, rows=rows),
        grid_spec=grid_spec,
        out_shape=jax.ShapeDtypeStruct((n_out_rows, D_MODEL), F32),
        compiler_params=_cparams(("arbitrary",)),
        name="moe_experts",
    )(blk_e, nvalid, tok3, tok3, slot3, x, w1, w3, w2)


def _combine_kernel(x_ref, xb_ref, wt_ref, y0, y1, y2, y3, y4, y5, s1_ref, s3_ref, s2_ref, g_ref, b_ref,
                    o_ref, ob_ref):
    wt = wt_ref[...]
    routed = None
    for k, yk in enumerate((y0, y1, y2, y3, y4, y5)):
        term = yk[...] * wt[:, k:k + 1]
        routed = term if routed is None else routed + term
    xb = xb_ref[...]
    hs = jax.nn.silu(_dot(xb, s1_ref[...])) * _dot(xb, s3_ref[...])
    shared = _dot(hs.astype(BF16), s2_ref[...])
    y = _ln_rows(ALPHA * x_ref[...] + (routed + shared), g_ref[...], b_ref[...])
    o_ref[...] = y
    ob_ref[...] = y.astype(BF16)


def _combine(x, xb, wt, y6, s1, s3, s2, g, b):
    t = x.shape[0]
    tm = _row_tile(t, 128)
    nt = t // tm
    c2 = lambda i: (0, 0)
    yspecs = [pl.BlockSpec((tm, D_MODEL), (lambda i, k=k: (k * nt + i, 0))) for k in range(TOP_K)]
    return pl.pallas_call(
        _combine_kernel,
        grid=(nt,),
        in_specs=[
            pl.BlockSpec((tm, D_MODEL), lambda i: (i, 0)),
            pl.BlockSpec((tm, D_MODEL), lambda i: (i, 0)),
            pl.BlockSpec((tm, LANES), lambda i: (i, 0)),
            *yspecs,
            pl.BlockSpec((D_MODEL, EXPERT_FF), c2),
            pl.BlockSpec((D_MODEL, EXPERT_FF), c2),
            pl.BlockSpec((EXPERT_FF, D_MODEL), c2),
            pl.BlockSpec((1, D_MODEL), c2),
            pl.BlockSpec((1, D_MODEL), c2),
        ],
        out_specs=[pl.BlockSpec((tm, D_MODEL), lambda i: (i, 0)),
                   pl.BlockSpec((tm, D_MODEL), lambda i: (i, 0))],
        out_shape=[jax.ShapeDtypeStruct((t, D_MODEL), F32), jax.ShapeDtypeStruct((t, D_MODEL), BF16)],
        compiler_params=_cparams(("parallel",)),
        name="moe_combine_ln2",
    )(x, xb, wt, y6, y6, y6, y6, y6, y6, s1, s3, s2, g.reshape(1, -1), b.reshape(1, -1))


def _route_plan(idx, t):
    rows = EXPERT_ROWS
    tk = t * TOP_K
    n_blocks = -(-(tk + N_EXPERTS * (rows - 1)) // rows)
    n_rows = n_blocks * rows
    flat_e = idx.reshape(-1)
    order = jnp.argsort(flat_e).astype(I32)
    counts = jnp.zeros((N_EXPERTS,), I32).at[flat_e].add(1)
    padded = (counts + rows - 1) // rows * rows
    pad_end = jnp.cumsum(padded)
    pad_start = pad_end - padded
    start = jnp.cumsum(counts) - counts
    blk = jnp.arange(n_blocks, dtype=I32)
    blk_e = jnp.minimum(jnp.sum(blk[:, None] * rows >= pad_end[None, :], axis=1), N_EXPERTS - 1).astype(I32)
    in_blk = jnp.arange(rows, dtype=I32)[None, :]
    j = blk[:, None] * rows + in_blk - pad_start[blk_e][:, None]
    valid = jnp.logical_and(j < counts[blk_e][:, None], (blk * rows < pad_end[N_EXPERTS - 1])[:, None])
    src = jnp.clip(start[blk_e][:, None] + j, 0, tk - 1)
    flat = order[src]
    tok = flat // TOP_K
    kk = flat - tok * TOP_K
    buf_tok = jnp.where(valid, tok, 0).astype(I32).reshape(-1)
    buf_slot = jnp.where(valid, kk * t + tok, 0).astype(I32).reshape(-1)
    nvalid = jnp.sum(valid, axis=1).astype(I32)
    return blk_e, nvalid, buf_tok, buf_slot


def _prep_layer(l, p):
    f32 = F32
    w_in = p["w_in"][l]
    offs = np.cumsum((0, 1024, 1024, 512, 512, 64, 1024))
    xa, ga, cq, ckv, kr, uc, gates = (w_in[:, offs[0]:offs[1]], w_in[:, offs[1]:offs[2]], w_in[:, offs[2]:offs[3]],
                                      w_in[:, offs[3]:offs[4]], w_in[:, offs[4]:offs[5]], w_in[:, offs[5]:offs[6]],
                                      w_in[:, offs[6]:])
    half = QK_ROPE // 2
    kr_sw = jnp.concatenate([kr[:, half:], kr[:, :half]], axis=1)
    w_in_aug = jnp.concatenate([xa, ga, cq, ckv, uc, gates, kr, kr_sw], axis=1).astype(BF16)

    wuq = p["mla_w_uq"][l]
    rope_w = wuq[:, :, QK_NOPE:]
    wuq_aug = jnp.concatenate([wuq, rope_w[:, :, half:], rope_w[:, :, :half]], axis=2)
    wuq_aug = jnp.transpose(wuq_aug, (1, 0, 2)).astype(BF16)
    wuk = jnp.transpose(p["mla_w_uk"][l], (1, 0, 2)).astype(BF16)
    wuv = jnp.transpose(p["mla_w_uv"][l], (1, 0, 2)).astype(BF16)

    wg = jnp.concatenate([p["lru_gate_a_w"][l], p["lru_gate_x_w"][l]], axis=-1).astype(BF16)
    bg = jnp.concatenate([p["lru_gate_a_b"][l], p["lru_gate_x_b"][l]], axis=-1)[:, None, :]
    sp = jax.nn.softplus(-p["lru_lambda"][l].astype(f32))

    dt = jnp.exp(p["s5_log_dt"][l].astype(f32))[:, None]
    lam_re = p["s5_lam_re"][l].astype(f32)
    lam_im = p["s5_lam_im"][l].astype(f32)
    mag = jnp.exp(lam_re * dt)
    a_re = mag * jnp.cos(lam_im * dt)
    a_im = mag * jnp.sin(lam_im * dt)
    den = jnp.square(lam_re) + jnp.square(lam_im)
    f_re = ((a_re - 1.0) * lam_re + a_im * lam_im) / den
    f_im = (a_im * lam_re - (a_re - 1.0) * lam_im) / den
    b_re = p["s5_b_re"][l].astype(f32)
    b_im = p["s5_b_im"][l].astype(f32)
    bb_re = f_re[..., None] * b_re - f_im[..., None] * b_im
    bb_im = f_re[..., None] * b_im + f_im[..., None] * b_re
    gpr = S5_GROUPS // S5_RG
    eye = jnp.eye(gpr, dtype=f32)

    def in_bd(bb):
        bbr = bb.reshape(S5_RG, gpr, S5_STATE, S5_GROUP)
        return jnp.einsum("rgpc,gh->rgchp", bbr, eye).reshape(S5_RG, gpr * S5_GROUP, gpr * S5_STATE).astype(BF16)

    def out_bd(cm):
        cr = cm.astype(f32).reshape(S5_RG, gpr, S5_GROUP, S5_STATE)
        return jnp.einsum("rgcp,gh->rgphc", cr, eye).reshape(S5_RG, gpr * S5_STATE, gpr * S5_GROUP).astype(BF16)

    s5 = dict(bbr=in_bd(bb_re), bbi=in_bd(bb_im), ccr=out_bd(p["s5_c_re"][l]), cci=out_bd(p["s5_c_im"][l]),
              ar=a_re.reshape(1, -1), ai=a_im.reshape(1, -1), d=p["s5_d"][l].astype(f32).reshape(1, -1),
              glu_w=p["s5_glu_w"][l].astype(BF16), glu_b=p["s5_glu_b"][l].reshape(1, -1))
    return dict(
        w_in=w_in_aug, wuq=wuq_aug, wuk=wuk, wuv=wuv, wg=wg, bg=bg, sp=sp, s5=s5,
        wa=p["w_branch_a"][l].astype(BF16), wb=p["w_branch_b"][l].astype(BF16), wc=p["w_branch_c"][l].astype(BF16),
        w_out=p["w_out"][l].astype(BF16), router_w=p["router_w"][l].astype(BF16),
        w1=p["exp_w1"][l], w3=p["exp_w3"][l], w2=p["exp_w2"][l],
        s1=p["sh_w1"][l].astype(BF16), s3=p["sh_w3"][l].astype(BF16), s2=p["sh_w2"][l].astype(BF16),
    )


def _rope_tables(bp, sp_len, bs, ss_len, past):
    half = QK_ROPE // 2
    inv = ROPE_THETA ** (-jnp.arange(half, dtype=F32) / half)
    pos = jnp.concatenate([jnp.tile(jnp.arange(sp_len, dtype=I32), bp),
                           jnp.tile(past + jnp.arange(ss_len, dtype=I32), bs)])
    ang = pos.astype(F32)[:, None] * inv
    cos, sin = jnp.cos(ang), jnp.sin(ang)
    return jnp.concatenate([cos, cos], axis=1), jnp.concatenate([-sin, sin], axis=1)


def _to_time_major(x, nb, seq_len):
    return x.reshape(nb, seq_len, -1).transpose(1, 0, 2).reshape(seq_len * nb, -1)


def _from_time_major(x, nb, seq_len):
    return x.reshape(seq_len, nb, -1).transpose(1, 0, 2).reshape(nb * seq_len, -1)


def kernel(x_prompt, x_sample, cache_mla_ckv, cache_mla_krope, state_lru_conv, state_lru_h, state_s5_re, state_s5_im, ln_in_g, ln_in_b, w_in, lru_conv_w, lru_conv_b, lru_gate_a_w, lru_gate_a_b, lru_gate_x_w, lru_gate_x_b, lru_lambda, mla_q_norm_g, mla_w_uq, mla_kv_norm_g, mla_w_uk, mla_w_uv, s5_lam_re, s5_lam_im, s5_log_dt, s5_b_re, s5_b_im, s5_c_re, s5_c_im, s5_d, s5_glu_w, s5_glu_b, w_branch_a, w_branch_b, w_branch_c, w_out, b_out, ln1_g, ln1_b, router_w, router_bias, exp_w1, exp_w3, exp_w2, sh_w1, sh_w3, sh_w2, ln2_g, ln2_b):
    p = dict(w_in=w_in, lru_gate_a_w=lru_gate_a_w, lru_gate_a_b=lru_gate_a_b, lru_gate_x_w=lru_gate_x_w,
             lru_gate_x_b=lru_gate_x_b, lru_lambda=lru_lambda, mla_w_uq=mla_w_uq, mla_w_uk=mla_w_uk,
             mla_w_uv=mla_w_uv, s5_lam_re=s5_lam_re, s5_lam_im=s5_lam_im, s5_log_dt=s5_log_dt, s5_b_re=s5_b_re,
             s5_b_im=s5_b_im, s5_c_re=s5_c_re, s5_c_im=s5_c_im, s5_d=s5_d, s5_glu_w=s5_glu_w, s5_glu_b=s5_glu_b,
             w_branch_a=w_branch_a, w_branch_b=w_branch_b, w_branch_c=w_branch_c, w_out=w_out, router_w=router_w,
             exp_w1=exp_w1, exp_w3=exp_w3, exp_w2=exp_w2, sh_w1=sh_w1, sh_w3=sh_w3, sh_w2=sh_w2)
    bp, sp_len, d = x_prompt.shape
    bs, ss_len, _ = x_sample.shape
    past = cache_mla_ckv.shape[2]
    depth = w_in.shape[0]
    assert d == D_MODEL and sp_len % CHUNK == 0 and ss_len == CHUNK
    tp, ts = bp * sp_len, bs * ss_len
    t = tp + ts

    cos2, sin2 = _rope_tables(bp, sp_len, bs, ss_len, past)
    x, xb = _ln_in(x_prompt.reshape(tp, d), x_sample.reshape(ts, d), ln_in_g, ln_in_b)

    new_p, new_s = [], []
    for l in range(depth):
        lp = _prep_layer(l, p)
        proj = _in_proj(xb, lp["w_in"])

        conv0 = jnp.concatenate([jnp.zeros((bp, CONV_WIDTH - 1, LRU_WIDTH), F32), state_lru_conv[l]], axis=0)
        h0 = jnp.concatenate([jnp.zeros((bp, LRU_WIDTH), F32), state_lru_h[l]], axis=0)[:, None, :]
        y_a, conv_new, h_new = _lru_branch(proj, conv0, h0, lru_conv_w[l], lru_conv_b[l], lp["wg"], lp["bg"],
                                           lp["sp"], bp=bp, sp_len=sp_len, bs=bs)
        h_new = h_new[:, 0, :]

        q = _q_proj(proj, mla_q_norm_g[l], lp["wuq"], cos2, sin2)
        ckv_new, kr_new, k, v = _kv_proj(proj, mla_kv_norm_g[l], lp["wuk"], lp["wuv"], cos2, sin2)
        kpast, vpast = _kv_cache_proj(cache_mla_ckv[l].reshape(bs * past, KV_LORA),
                                      cache_mla_krope[l].reshape(bs * past, QK_ROPE), lp["wuk"], lp["wuv"])
        yb_p = _attn_prompt(q, k, v, bp=bp, sp_len=sp_len)
        yb_s = _attn_sample(q, k, v, kpast, vpast, bs=bs, ss_len=ss_len, past=past, tp=tp)

        uc = proj[:, COL_UC:COL_UC + S5_WIDTH]
        zero_state = jnp.zeros((bp, S5_HID), F32)
        o_p, s5r_p, s5i_p = _s5_branch(_to_time_major(uc[:tp], bp, sp_len), zero_state, zero_state, lp["s5"],
                                       nb=bp, seq_len=sp_len)
        o_s, s5r_s, s5i_s = _s5_branch(_to_time_major(uc[tp:], bs, ss_len), state_s5_re[l].reshape(bs, S5_HID),
                                       state_s5_im[l].reshape(bs, S5_HID), lp["s5"], nb=bs, seq_len=ss_len)
        y_c = jnp.concatenate([_from_time_major(o_p, bp, sp_len), _from_time_major(o_s, bs, ss_len)], axis=0)

        merged = _merge(y_a, yb_p, yb_s, y_c, proj, lp["wa"], lp["wb"], lp["wc"])
        x, xb = _out_ln(x, merged, lp["w_out"], b_out[l], ln1_g[l], ln1_b[l])

        idx128, wt128 = _router(xb, lp["router_w"], router_bias[l])
        blk_e, nvalid, buf_tok, buf_slot = _route_plan(idx128[:, :TOP_K], t)
        y6 = _experts(x, lp["w1"], lp["w3"], lp["w2"], blk_e, nvalid, buf_tok, buf_slot, n_out_rows=TOP_K * t)
        x, xb = _combine(x, xb, wt128, y6, lp["s1"], lp["s3"], lp["s2"], ln2_g[l], ln2_b[l])

        new_p.append((ckv_new[:tp].reshape(bp, sp_len, KV_LORA), kr_new[:tp].reshape(bp, sp_len, QK_ROPE),
                      conv_new[:bp], h_new[:bp], s5r_p.reshape(bp, S5_GROUPS, S5_STATE),
                      s5i_p.reshape(bp, S5_GROUPS, S5_STATE)))
        new_s.append((ckv_new[tp:].reshape(bs, ss_len, KV_LORA), kr_new[tp:].reshape(bs, ss_len, QK_ROPE),
                      conv_new[bp:], h_new[bp:], s5r_s.reshape(bs, S5_GROUPS, S5_STATE),
                      s5i_s.reshape(bs, S5_GROUPS, S5_STATE)))

    p_out = [jnp.stack(z) for z in zip(*new_p)]
    s_out = [jnp.stack(z) for z in zip(*new_s)]
    return (x[:tp].reshape(bp, sp_len, d), x[tp:].reshape(bs, ss_len, d), *p_out, *s_out)
```

```python
import functools
import math

import jax
import jax.numpy as jnp
import numpy as np
from jax import lax
from jax.experimental import pallas as pl
from jax.experimental.pallas import tpu as pltpu

F32 = jnp.float32
BF16 = jnp.bfloat16
I32 = jnp.int32

D_MODEL = 2048
DEPTH = 2
CHUNK = 64
LN_EPS = 1e-5
RMS_EPS = 1e-6
ALPHA = (2 * DEPTH) ** 0.25
LRU_WIDTH = 1024
LRU_BLOCKS = 8
LRU_BLOCK = 128
CONV_WIDTH = 4
LRU_C = 8.0
MLA_HEADS = 16
QK_NOPE = 128
QK_ROPE = 64
QK_DIM = QK_NOPE + QK_ROPE
V_HEAD = 128
Q_LORA = 512
KV_LORA = 512
ROPE_THETA = 10000.0
S5_WIDTH = 1024
S5_GROUP = 16
S5_GROUPS = 64
S5_STATE = 64
S5_HID = S5_GROUPS * S5_STATE
N_EXPERTS = 64
TOP_K = 6
N_GROUPS = 8
GROUP_SIZE = N_EXPERTS // N_GROUPS
TOPK_GROUPS = 4
EXPERT_FF = 512
ROUTE_SCALE = 2.5

COL_XA, COL_GA, COL_CQ, COL_CKV, COL_UC, COL_GATES, COL_KR = 0, 1024, 2048, 2560, 3072, 4096, 10240
N_PROJ = 10368

V7X_VMEM_BYTES = 64 * 1024 * 1024
VMEM_LIMIT = V7X_VMEM_BYTES - 8 * 1024 * 1024
LANES = 128
SUBLANES = 8

EXPERT_ROWS = 256
ATTN_TQ = 1024
ATTN_HEADS_PER_STEP = 2


def _cparams(sem):
    return pltpu.CompilerParams(dimension_semantics=sem, vmem_limit_bytes=VMEM_LIMIT)


def _row_tile(n, pref):
    for t in (1024, 512, 256, 128, 64, 32, 16, 8):
        if t <= pref and n % t == 0:
            return t
    raise ValueError(f"no row tile for {n}")


def _ln_rows(x, g, b):
    mu = jnp.mean(x, axis=-1, keepdims=True)
    xc = x - mu
    var = jnp.mean(xc * xc, axis=-1, keepdims=True)
    return xc * lax.rsqrt(var + LN_EPS) * g + b


def _dot(a, b):
    return jnp.dot(a, b, preferred_element_type=F32)


def _ln_in_kernel(xp_ref, xs_ref, g_ref, b_ref, o_ref, ob_ref, *, n_prompt_tiles):
    i = pl.program_id(0)

    def emit(x):
        y = _ln_rows(x, g_ref[...], b_ref[...])
        o_ref[...] = y
        ob_ref[...] = y.astype(BF16)

    @pl.when(i < n_prompt_tiles)
    def _():
        emit(xp_ref[...])

    @pl.when(i >= n_prompt_tiles)
    def _():
        emit(xs_ref[...])


def _ln_in(xp, xs, g, b):
    tp, ts = xp.shape[0], xs.shape[0]
    tm = _row_tile(math.gcd(tp, ts), 512)
    npt, nst = tp // tm, ts // tm
    t = tp + ts
    return pl.pallas_call(
        functools.partial(_ln_in_kernel, n_prompt_tiles=npt),
        grid=(npt + nst,),
        in_specs=[
            pl.BlockSpec((tm, D_MODEL), lambda i: (jnp.minimum(i, npt - 1), 0)),
            pl.BlockSpec((tm, D_MODEL), lambda i: (jnp.maximum(i - npt, 0), 0)),
            pl.BlockSpec((1, D_MODEL), lambda i: (0, 0)),
            pl.BlockSpec((1, D_MODEL), lambda i: (0, 0)),
        ],
        out_specs=[pl.BlockSpec((tm, D_MODEL), lambda i: (i, 0)),
                   pl.BlockSpec((tm, D_MODEL), lambda i: (i, 0))],
        out_shape=[jax.ShapeDtypeStruct((t, D_MODEL), F32), jax.ShapeDtypeStruct((t, D_MODEL), BF16)],
        compiler_params=_cparams(("parallel",)),
        name="ln_in",
    )(xp, xs, g.reshape(1, -1), b.reshape(1, -1))


def _mm_kernel(x_ref, w_ref, o_ref):
    o_ref[...] = _dot(x_ref[...], w_ref[...])


def _in_proj(xb, w):
    t, k = xb.shape
    n = w.shape[1]
    tm = _row_tile(t, 512)
    tn = 1152
    assert n % tn == 0
    return pl.pallas_call(
        _mm_kernel,
        grid=(n // tn, t // tm),
        in_specs=[pl.BlockSpec((tm, k), lambda j, i: (i, 0)),
                  pl.BlockSpec((k, tn), lambda j, i: (0, j))],
        out_specs=pl.BlockSpec((tm, tn), lambda j, i: (i, j)),
        out_shape=jax.ShapeDtypeStruct((t, n), F32),
        compiler_params=_cparams(("parallel", "parallel")),
        name="in_proj",
    )(xb, w)


def _lru_kernel(xa_ref, ga_ref, cs_ref, h0_ref, cw_ref, cb_ref, wg_ref, bg_ref, sp_ref,
                y_ref, cn_ref, hn_ref, xbuf, hcar, *, rows, n_prompt_tiles, tiles_per_seq):
    c = pl.program_id(0)
    first = jnp.logical_or(c >= n_prompt_tiles, c % tiles_per_seq == 0)

    @pl.when(first)
    def _():
        xbuf[5:8, :] = cs_ref[0]
        hcar[0:1, :] = h0_ref[0]

    xa = xa_ref[...]
    xbuf[8:8 + rows, :] = xa
    cw = cw_ref[...]
    xc = (cb_ref[...] + cw[0:1] * xbuf[5:5 + rows, :] + cw[1:2] * xbuf[6:6 + rows, :]
          + cw[2:3] * xbuf[7:7 + rows, :] + cw[3:4] * xa)
    tail = xa[rows - 3:rows, :]
    xbuf[5:8, :] = tail
    cn_ref[0] = tail

    row = lax.broadcasted_iota(I32, (rows, LRU_BLOCK), 0)
    for n in range(LRU_BLOCKS):
        sl = slice(n * LRU_BLOCK, (n + 1) * LRU_BLOCK)
        xcb = xc[:, sl]
        g = _dot(xcb.astype(BF16), wg_ref[n]) + bg_ref[n]
        r = jax.nn.sigmoid(g[:, :LRU_BLOCK])
        gi = jax.nn.sigmoid(g[:, LRU_BLOCK:])
        log_a = -LRU_C * r * sp_ref[:, sl]
        a = jnp.exp(log_a)
        th = jnp.tanh(log_a)
        b = jnp.sqrt(-2.0 * th / (1.0 - th)) * (gi * xcb)
        s = 1
        while s < rows:
            keep = row >= s
            a_s = jnp.where(keep, pltpu.roll(a, s, 0), 1.0)
            b_s = jnp.where(keep, pltpu.roll(b, s, 0), 0.0)
            b = a * b_s + b
            a = a * a_s
            s *= 2
        h = a * hcar[0:1, sl] + b
        hcar[0:1, sl] = h[rows - 1:rows, :]
        y_ref[:, sl] = (h * jax.nn.gelu(ga_ref[:, sl])).astype(BF16)
    hn_ref[0] = hcar[0:1, :]


def _lru_branch(proj, conv_state, h_state, conv_w, conv_b, wg, bg, sp, *, bp, sp_len, bs):
    t = proj.shape[0]
    rows = CHUNK
    tps = sp_len // rows
    npt = bp * tps
    nt = t // rows
    nseq = bp + bs
    w = LRU_WIDTH

    def seq_of(c):
        return jnp.where(c < npt, c // tps, bp + (c - npt))

    return pl.pallas_call(
        functools.partial(_lru_kernel, rows=rows, n_prompt_tiles=npt, tiles_per_seq=tps),
        grid=(nt,),
        in_specs=[
            pl.BlockSpec((rows, w), lambda c: (c, COL_XA // w)),
            pl.BlockSpec((rows, w), lambda c: (c, COL_GA // w)),
            pl.BlockSpec((1, 3, w), lambda c: (seq_of(c), 0, 0)),
            pl.BlockSpec((1, 1, w), lambda c: (seq_of(c), 0, 0)),
            pl.BlockSpec((CONV_WIDTH, w), lambda c: (0, 0)),
            pl.BlockSpec((1, w), lambda c: (0, 0)),
            pl.BlockSpec((LRU_BLOCKS, LRU_BLOCK, 2 * LRU_BLOCK), lambda c: (0, 0, 0)),
            pl.BlockSpec((LRU_BLOCKS, 1, 2 * LRU_BLOCK), lambda c: (0, 0, 0)),
            pl.BlockSpec((1, w), lambda c: (0, 0)),
        ],
        out_specs=[
            pl.BlockSpec((rows, w), lambda c: (c, 0)),
            pl.BlockSpec((1, 3, w), lambda c: (seq_of(c), 0, 0)),
            pl.BlockSpec((1, 1, w), lambda c: (seq_of(c), 0, 0)),
        ],
        out_shape=[jax.ShapeDtypeStruct((t, w), BF16),
                   jax.ShapeDtypeStruct((nseq, 3, w), F32),
                   jax.ShapeDtypeStruct((nseq, 1, w), F32)],
        scratch_shapes=[pltpu.VMEM((rows + 8, w), F32), pltpu.VMEM((SUBLANES, w), F32)],
        compiler_params=_cparams(("arbitrary",)),
        name="rglru",
    )(proj, proj, conv_state, h_state, conv_w, conv_b.reshape(1, -1), wg, bg, sp.reshape(1, -1))


S5_COLS = 512
S5_RG = 4


def _s5_kernel(*refs, nb, steps):
    u_refs = refs[:nb]
    (h0r_ref, h0i_ref, bbr_ref, bbi_ref, ccr_ref, cci_ref, ar_ref, ai_ref, d_ref, gw_ref, gb_ref,
     o_ref, hr_out, hi_out, bur, bui, hr, hi, tm_in, tm_out) = refs[nb:]
    i = pl.program_id(0)
    n_slab = S5_WIDTH // LANES

    @pl.when(i == 0)
    def _():
        hr[...] = h0r_ref[...]
        hi[...] = h0i_ref[...]

    for b in range(nb):
        ub = u_refs[b][...]
        for s in range(n_slab):
            tm_in[s, pl.ds(b, steps, stride=nb), :] = ub[:, s * LANES:(s + 1) * LANES]
    u = jnp.concatenate([tm_in[s] for s in range(n_slab)], axis=-1)
    ub = u.astype(BF16)
    kin = S5_WIDTH // S5_RG
    kst = S5_HID // S5_RG
    for r in range(S5_RG):
        ur = ub[:, r * kin:(r + 1) * kin]
        bur[:, r * kst:(r + 1) * kst] = _dot(ur, bbr_ref[r])
        bui[:, r * kst:(r + 1) * kst] = _dot(ur, bbi_ref[r])

    for cc in range(S5_HID // S5_COLS):
        cs = slice(cc * S5_COLS, (cc + 1) * S5_COLS)
        ar = jnp.broadcast_to(ar_ref[:, cs], (SUBLANES, S5_COLS))
        ai = jnp.broadcast_to(ai_ref[:, cs], (SUBLANES, S5_COLS))
        for bg in range(nb // SUBLANES):
            bsl = slice(bg * SUBLANES, (bg + 1) * SUBLANES)

            def body(t, carry, cs=cs, bg=bg, ar=ar, ai=ai):
                h_r, h_i = carry
                rsl = pl.ds(pl.multiple_of(t * nb + bg * SUBLANES, SUBLANES), SUBLANES)
                n_r = ar * h_r - ai * h_i + bur[rsl, cs]
                n_i = ar * h_i + ai * h_r + bui[rsl, cs]
                bur[rsl, cs] = n_r
                bui[rsl, cs] = n_i
                return n_r, n_i

            h_r, h_i = lax.fori_loop(0, steps, body, (hr[bsl, cs], hi[bsl, cs]))
            hr[bsl, cs] = h_r
            hi[bsl, cs] = h_i

    ys = []
    for r in range(S5_RG):
        hrb = bur[:, r * kst:(r + 1) * kst].astype(BF16)
        hib = bui[:, r * kst:(r + 1) * kst].astype(BF16)
        ys.append(_dot(hrb, ccr_ref[r]) - _dot(hib, cci_ref[r]))
    y = jnp.concatenate(ys, axis=-1) + d_ref[...] * u
    z = jax.nn.gelu(y)
    gate = jax.nn.sigmoid(_dot(z.astype(BF16), gw_ref[...]) + gb_ref[...])
    out = z * gate
    for s in range(n_slab):
        tm_out[s] = out[:, s * LANES:(s + 1) * LANES]
    for b in range(nb):
        ob = jnp.concatenate([tm_out[s, pl.ds(b, steps, stride=nb), :] for s in range(n_slab)], axis=-1)
        o_ref[b] = ob.astype(BF16)
    hr_out[...] = hr[...]
    hi_out[...] = hi[...]


def _s5_branch(proj, row_off, h0r, h0i, pw, *, nb, seq_len):
    assert nb % SUBLANES == 0
    steps = max(16, 256 // nb)
    assert seq_len % steps == 0 and row_off % steps == 0
    rows = steps * nb
    kin = S5_WIDTH // S5_RG
    kst = S5_HID // S5_RG
    n_slab = S5_WIDTH // LANES
    const2 = lambda i: (0, 0)
    const3 = lambda i: (0, 0, 0)
    tiles_per_seq = seq_len // steps
    u_specs = [pl.BlockSpec((steps, S5_WIDTH),
                            (lambda i, b=b: (row_off // steps + b * tiles_per_seq + i, COL_UC // S5_WIDTH)))
               for b in range(nb)]
    return pl.pallas_call(
        functools.partial(_s5_kernel, nb=nb, steps=steps),
        grid=(tiles_per_seq,),
        in_specs=[
            *u_specs,
            pl.BlockSpec((nb, S5_HID), const2),
            pl.BlockSpec((nb, S5_HID), const2),
            pl.BlockSpec((S5_RG, kin, kst), const3),
            pl.BlockSpec((S5_RG, kin, kst), const3),
            pl.BlockSpec((S5_RG, kst, kin), const3),
            pl.BlockSpec((S5_RG, kst, kin), const3),
            pl.BlockSpec((1, S5_HID), const2),
            pl.BlockSpec((1, S5_HID), const2),
            pl.BlockSpec((1, S5_WIDTH), const2),
            pl.BlockSpec((S5_WIDTH, S5_WIDTH), const2),
            pl.BlockSpec((1, S5_WIDTH), const2),
        ],
        out_specs=[pl.BlockSpec((nb, steps, S5_WIDTH), lambda i: (0, i, 0)),
                   pl.BlockSpec((nb, S5_HID), const2),
                   pl.BlockSpec((nb, S5_HID), const2)],
        out_shape=[jax.ShapeDtypeStruct((nb, seq_len, S5_WIDTH), BF16),
                   jax.ShapeDtypeStruct((nb, S5_HID), F32),
                   jax.ShapeDtypeStruct((nb, S5_HID), F32)],
        scratch_shapes=[pltpu.VMEM((rows, S5_HID), F32), pltpu.VMEM((rows, S5_HID), F32),
                        pltpu.VMEM((nb, S5_HID), F32), pltpu.VMEM((nb, S5_HID), F32),
                        pltpu.VMEM((n_slab, rows, LANES), F32), pltpu.VMEM((n_slab, rows, LANES), F32)],
        compiler_params=_cparams(("arbitrary",)),
        name="s5",
    )(*([proj] * nb), h0r, h0i, pw["bbr"], pw["bbi"], pw["ccr"], pw["cci"], pw["ar"], pw["ai"], pw["d"],
      pw["glu_w"], pw["glu_b"])


def _rms_rows(x, g):
    return x * lax.rsqrt(jnp.mean(x * x, axis=-1, keepdims=True) + RMS_EPS) * g


def _qproj_kernel(cq_ref, g_ref, w_ref, cos_ref, sin_ref, q_ref):
    xn = _rms_rows(cq_ref[...], g_ref[...]).astype(BF16)
    cos2 = cos_ref[...]
    sin2 = sin_ref[...]
    for h in range(MLA_HEADS):
        r = _dot(xn, w_ref[h])
        q_ref[h, :, 0:QK_NOPE] = r[:, 0:QK_NOPE].astype(BF16)
        rot = r[:, QK_NOPE:QK_DIM] * cos2 + r[:, QK_DIM:QK_DIM + QK_ROPE] * sin2
        q_ref[h, :, QK_NOPE:QK_DIM] = rot.astype(BF16)


def _q_proj(proj, g, w_aug, cos2, sin2):
    t = proj.shape[0]
    tm = _row_tile(t, 256)
    return pl.pallas_call(
        _qproj_kernel,
        grid=(t // tm,),
        in_specs=[
            pl.BlockSpec((tm, Q_LORA), lambda i: (i, COL_CQ // Q_LORA)),
            pl.BlockSpec((1, Q_LORA), lambda i: (0, 0)),
            pl.BlockSpec((MLA_HEADS, Q_LORA, 2 * LANES), lambda i: (0, 0, 0)),
            pl.BlockSpec((tm, QK_ROPE), lambda i: (i, 0)),
            pl.BlockSpec((tm, QK_ROPE), lambda i: (i, 0)),
        ],
        out_specs=pl.BlockSpec((MLA_HEADS, tm, QK_DIM), lambda i: (0, i, 0)),
        out_shape=jax.ShapeDtypeStruct((MLA_HEADS, t, QK_DIM), BF16),
        compiler_params=_cparams(("parallel",)),
        name="mla_q_proj",
    )(proj, g.reshape(1, -1), w_aug, cos2, sin2)


def _kvproj_kernel(ckv_ref, kr_ref, g_ref, wk_ref, wv_ref, cos_ref, sin_ref,
                   ckvn_ref, krn_ref, k_ref, v_ref):
    xn = _rms_rows(ckv_ref[...], g_ref[...])
    ckvn_ref[...] = xn
    xb = xn.astype(BF16)
    kr = kr_ref[...]
    rot = kr[:, 0:QK_ROPE] * cos_ref[...] + kr[:, QK_ROPE:2 * QK_ROPE] * sin_ref[...]
    krn_ref[...] = rot
    rb = rot.astype(BF16)
    for h in range(MLA_HEADS):
        k_ref[h, :, 0:QK_NOPE] = _dot(xb, wk_ref[h]).astype(BF16)
        k_ref[h, :, QK_NOPE:QK_DIM] = rb
        v_ref[h] = _dot(xb, wv_ref[h]).astype(BF16)


def _kv_proj(proj, g, wk, wv, cos2, sin2):
    t = proj.shape[0]
    tm = _row_tile(t, 256)
    c3 = lambda i: (0, 0, 0)
    return pl.pallas_call(
        _kvproj_kernel,
        grid=(t // tm,),
        in_specs=[
            pl.BlockSpec((tm, KV_LORA), lambda i: (i, COL_CKV // KV_LORA)),
            pl.BlockSpec((tm, LANES), lambda i: (i, COL_KR // LANES)),
            pl.BlockSpec((1, KV_LORA), lambda i: (0, 0)),
            pl.BlockSpec((MLA_HEADS, KV_LORA, QK_NOPE), c3),
            pl.BlockSpec((MLA_HEADS, KV_LORA, V_HEAD), c3),
            pl.BlockSpec((tm, QK_ROPE), lambda i: (i, 0)),
            pl.BlockSpec((tm, QK_ROPE), lambda i: (i, 0)),
        ],
        out_specs=[
            pl.BlockSpec((tm, KV_LORA), lambda i: (i, 0)),
            pl.BlockSpec((tm, QK_ROPE), lambda i: (i, 0)),
            pl.BlockSpec((MLA_HEADS, tm, QK_DIM), lambda i: (0, i, 0)),
            pl.BlockSpec((MLA_HEADS, tm, V_HEAD), lambda i: (0, i, 0)),
        ],
        out_shape=[jax.ShapeDtypeStruct((t, KV_LORA), F32),
                   jax.ShapeDtypeStruct((t, QK_ROPE), F32),
                   jax.ShapeDtypeStruct((MLA_HEADS, t, QK_DIM), BF16),
                   jax.ShapeDtypeStruct((MLA_HEADS, t, V_HEAD), BF16)],
        compiler_params=_cparams(("parallel",)),
        name="mla_kv_proj",
    )(proj, proj, g.reshape(1, -1), wk, wv, cos2, sin2)


def _kvcache_kernel(ckv_ref, kr_ref, wk_ref, wv_ref, k_ref, v_ref):
    xb = ckv_ref[...].astype(BF16)
    rb = kr_ref[...].astype(BF16)
    for h in range(MLA_HEADS):
        k_ref[h, :, 0:QK_NOPE] = _dot(xb, wk_ref[h]).astype(BF16)
        k_ref[h, :, QK_NOPE:QK_DIM] = rb
        v_ref[h] = _dot(xb, wv_ref[h]).astype(BF16)


def _kv_cache_proj(ckv, kr, wk, wv):
    t = ckv.shape[0]
    tm = _row_tile(t, 256)
    c3 = lambda i: (0, 0, 0)
    return pl.pallas_call(
        _kvcache_kernel,
        grid=(t // tm,),
        in_specs=[
            pl.BlockSpec((tm, KV_LORA), lambda i: (i, 0)),
            pl.BlockSpec((tm, QK_ROPE), lambda i: (i, 0)),
            pl.BlockSpec((MLA_HEADS, KV_LORA, QK_NOPE), c3),
            pl.BlockSpec((MLA_HEADS, KV_LORA, V_HEAD), c3),
        ],
        out_specs=[pl.BlockSpec((MLA_HEADS, tm, QK_DIM), lambda i: (0, i, 0)),
                   pl.BlockSpec((MLA_HEADS, tm, V_HEAD), lambda i: (0, i, 0))],
        out_shape=[jax.ShapeDtypeStruct((MLA_HEADS, t, QK_DIM), BF16),
                   jax.ShapeDtypeStruct((MLA_HEADS, t, V_HEAD), BF16)],
        compiler_params=_cparams(("parallel",)),
        name="mla_kv_cache_proj",
    )(ckv, kr, wk, wv)


def _qk(q, k):
    return lax.dot_general(q, k, (((1,), (1,)), ((), ())), preferred_element_type=F32)


def _attn_prompt_kernel(q_ref, k_ref, v_ref, o_ref, *, tq, scale):
    qi = pl.program_id(2)
    c = scale * math.log2(math.e)
    nh = q_ref.shape[0]
    qs = [q_ref[h] for h in range(nh)]

    def kv_rows(ki):
        return pl.ds(pl.multiple_of(ki * tq, tq), tq)

    def scores(h, ki):
        return _qk(qs[h], k_ref[h, kv_rows(ki), :])

    def update(state, s, v):
        m, l, acc = state
        m_new = jnp.maximum(m, jnp.max(s, axis=-1, keepdims=True))
        alpha = jnp.exp2((m - m_new) * c)
        p = jnp.exp2((s - m_new) * c)
        l = alpha * l + jnp.sum(p, axis=-1, keepdims=True)
        acc = alpha * acc + _dot(p.astype(BF16), v)
        return m_new, l, acc

    def body(ki, states):
        return tuple(update(states[h], scores(h, ki), v_ref[h, kv_rows(ki), :]) for h in range(nh))

    init = tuple((jnp.full((tq, 1), -jnp.inf, F32), jnp.zeros((tq, 1), F32), jnp.zeros((tq, V_HEAD), F32))
                 for _ in range(nh))
    states = lax.fori_loop(0, qi, body, init)
    rr = lax.broadcasted_iota(I32, (tq, tq), 0) // CHUNK
    cc = lax.broadcasted_iota(I32, (tq, tq), 1) // CHUNK
    for h in range(nh):
        s = jnp.where(cc <= rr, scores(h, qi), -jnp.inf)
        _, l, acc = update(states[h], s, v_ref[h, kv_rows(qi), :])
        o_ref[:, h * V_HEAD:(h + 1) * V_HEAD] = (acc / l).astype(BF16)


def _attn_prompt(q, k, v, *, bp, sp_len):
    tq = _row_tile(sp_len, ATTN_TQ)
    nq = sp_len // tq
    nh = ATTN_HEADS_PER_STEP
    scale = QK_DIM ** -0.5
    return pl.pallas_call(
        functools.partial(_attn_prompt_kernel, tq=tq, scale=scale),
        grid=(bp, MLA_HEADS // nh, nq),
        in_specs=[
            pl.BlockSpec((nh, tq, QK_DIM), lambda b, h, i: (h, b * nq + i, 0)),
            pl.BlockSpec((nh, sp_len, QK_DIM), lambda b, h, i: (h, b, 0)),
            pl.BlockSpec((nh, sp_len, V_HEAD), lambda b, h, i: (h, b, 0)),
        ],
        out_specs=pl.BlockSpec((tq, nh * V_HEAD), lambda b, h, i: (b * nq + i, h)),
        out_shape=jax.ShapeDtypeStruct((bp * sp_len, MLA_HEADS * V_HEAD), BF16),
        compiler_params=_cparams(("parallel", "parallel", "arbitrary")),
        name="mla_attn_prompt",
    )(q, k, v)


def _attn_sample_kernel(q_ref, kp_ref, vp_ref, kn_ref, vn_ref, o_ref, *, scale):
    for h in range(MLA_HEADS):
        q = q_ref[h]
        s1 = _qk(q, kp_ref[h]) * scale
        s2 = _qk(q, kn_ref[h]) * scale
        m = jnp.maximum(jnp.max(s1, axis=-1, keepdims=True), jnp.max(s2, axis=-1, keepdims=True))
        p1 = jnp.exp(s1 - m)
        p2 = jnp.exp(s2 - m)
        l = jnp.sum(p1, axis=-1, keepdims=True) + jnp.sum(p2, axis=-1, keepdims=True)
        acc = _dot(p1.astype(BF16), vp_ref[h]) + _dot(p2.astype(BF16), vn_ref[h])
        o_ref[:, h * V_HEAD:(h + 1) * V_HEAD] = (acc / l).astype(BF16)


def _attn_sample(q, k, v, kpast, vpast, *, bs, ss_len, past, tp):
    assert ss_len == CHUNK and past % CHUNK == 0 and tp % ss_len == 0
    off = tp // ss_len
    scale = QK_DIM ** -0.5
    hh = MLA_HEADS
    return pl.pallas_call(
        functools.partial(_attn_sample_kernel, scale=scale),
        grid=(bs,),
        in_specs=[
            pl.BlockSpec((hh, ss_len, QK_DIM), lambda b: (0, off + b, 0)),
            pl.BlockSpec((hh, past, QK_DIM), lambda b: (0, b, 0)),
            pl.BlockSpec((hh, past, V_HEAD), lambda b: (0, b, 0)),
            pl.BlockSpec((hh, ss_len, QK_DIM), lambda b: (0, off + b, 0)),
            pl.BlockSpec((hh, ss_len, V_HEAD), lambda b: (0, off + b, 0)),
        ],
        out_specs=pl.BlockSpec((ss_len, hh * V_HEAD), lambda b: (b, 0)),
        out_shape=jax.ShapeDtypeStruct((bs * ss_len, hh * V_HEAD), BF16),
        compiler_params=_cparams(("parallel",)),
        name="mla_attn_sample",
    )(q, kpast, vpast, k, v)


def _merge_kernel(ya_ref, ybp_ref, ybs_ref, ycp_ref, ycs_ref, ga_ref, gb_ref, gc_ref, wa_ref, wb_ref, wc_ref,
                  o_ref, *, n_prompt_tiles):
    i = pl.program_id(1)

    def emit(yb, yc):
        m = jax.nn.sigmoid(ga_ref[...]) * _dot(ya_ref[...], wa_ref[...])
        m = m + jax.nn.sigmoid(gb_ref[...]) * _dot(yb, wb_ref[...])
        m = m + jax.nn.sigmoid(gc_ref[...]) * _dot(yc, wc_ref[...])
        o_ref[...] = m.astype(BF16)

    @pl.when(i < n_prompt_tiles)
    def _():
        emit(ybp_ref[...], ycp_ref[...])

    @pl.when(i >= n_prompt_tiles)
    def _():
        emit(ybs_ref[...], ycs_ref[...])


def _merge(ya, yb_p, yb_s, yc_p, yc_s, proj, wa, wb, wc):
    t = ya.shape[0]
    tp, ts = yb_p.shape[0], yb_s.shape[0]
    tm = _row_tile(math.gcd(tp, ts), 256)
    npt = tp // tm
    tn = 1024
    nn = D_MODEL // tn
    g0 = COL_GATES // tn
    return pl.pallas_call(
        functools.partial(_merge_kernel, n_prompt_tiles=npt),
        grid=(nn, t // tm),
        in_specs=[
            pl.BlockSpec((tm, LRU_WIDTH), lambda j, i: (i, 0)),
            pl.BlockSpec((tm, MLA_HEADS * V_HEAD), lambda j, i: (jnp.minimum(i, npt - 1), 0)),
            pl.BlockSpec((tm, MLA_HEADS * V_HEAD), lambda j, i: (jnp.maximum(i - npt, 0), 0)),
            pl.BlockSpec((tm, S5_WIDTH), lambda j, i: (jnp.minimum(i, npt - 1), 0)),
            pl.BlockSpec((tm, S5_WIDTH), lambda j, i: (jnp.maximum(i - npt, 0), 0)),
            pl.BlockSpec((tm, tn), lambda j, i: (i, g0 + j)),
            pl.BlockSpec((tm, tn), lambda j, i: (i, g0 + nn + j)),
            pl.BlockSpec((tm, tn), lambda j, i: (i, g0 + 2 * nn + j)),
            pl.BlockSpec((LRU_WIDTH, tn), lambda j, i: (0, j)),
            pl.BlockSpec((MLA_HEADS * V_HEAD, tn), lambda j, i: (0, j)),
            pl.BlockSpec((S5_WIDTH, tn), lambda j, i: (0, j)),
        ],
        out_specs=pl.BlockSpec((tm, tn), lambda j, i: (i, j)),
        out_shape=jax.ShapeDtypeStruct((t, D_MODEL), BF16),
        compiler_params=_cparams(("parallel", "parallel")),
        name="branch_merge",
    )(ya, yb_p, yb_s, yc_p, yc_s, proj, proj, proj, wa, wb, wc)


def _outln_kernel(x_ref, m_ref, w_ref, bo_ref, g_ref, b_ref, o_ref, ob_ref):
    y = ALPHA * x_ref[...] + _dot(m_ref[...], w_ref[...]) + bo_ref[...]
    y = _ln_rows(y, g_ref[...], b_ref[...])
    o_ref[...] = y
    ob_ref[...] = y.astype(BF16)


def _out_ln(x, merged, w_out, b_out, g, b):
    t = x.shape[0]
    tm = _row_tile(t, 256)
    c2 = lambda i: (0, 0)
    return pl.pallas_call(
        _outln_kernel,
        grid=(t // tm,),
        in_specs=[
            pl.BlockSpec((tm, D_MODEL), lambda i: (i, 0)),
            pl.BlockSpec((tm, D_MODEL), lambda i: (i, 0)),
            pl.BlockSpec((D_MODEL, D_MODEL), c2),
            pl.BlockSpec((1, D_MODEL), c2),
            pl.BlockSpec((1, D_MODEL), c2),
            pl.BlockSpec((1, D_MODEL), c2),
        ],
        out_specs=[pl.BlockSpec((tm, D_MODEL), lambda i: (i, 0)),
                   pl.BlockSpec((tm, D_MODEL), lambda i: (i, 0))],
        out_shape=[jax.ShapeDtypeStruct((t, D_MODEL), F32), jax.ShapeDtypeStruct((t, D_MODEL), BF16)],
        compiler_params=_cparams(("parallel",)),
        name="out_proj_ln1",
    )(x, merged, w_out, b_out.reshape(1, -1), g.reshape(1, -1), b.reshape(1, -1))


def _router_kernel(x_ref, wt_ref, bias_ref, idx_ref, wgt_ref):
    tm = x_ref.shape[0]
    logits = lax.dot_general(wt_ref[...], x_ref[...], (((1,), (1,)), ((), ())),
                             preferred_element_type=F32)
    neg = -jnp.inf
    sub = lax.broadcasted_iota(I32, (GROUP_SIZE, tm), 0)
    scores, choice = [], []
    for g in range(N_GROUPS):
        sg = jax.nn.sigmoid(logits[g * GROUP_SIZE:(g + 1) * GROUP_SIZE, :])
        scores.append(sg)
        choice.append(sg + bias_ref[g * GROUP_SIZE:(g + 1) * GROUP_SIZE, :])

    def colmax(v):
        return jnp.max(v, axis=0, keepdims=True)

    gscore = []
    for g in range(N_GROUPS):
        m1 = colmax(choice[g])
        j1 = jnp.min(jnp.where(choice[g] == m1, sub, GROUP_SIZE), axis=0, keepdims=True)
        m2 = colmax(jnp.where(sub == j1, neg, choice[g]))
        gscore.append(m1 + m2)
    taken = [jnp.zeros((1, tm), jnp.bool_) for _ in range(N_GROUPS)]
    for _ in range(TOPK_GROUPS):
        best = None
        for g in range(N_GROUPS):
            cand = jnp.where(taken[g], neg, gscore[g])
            best = cand if best is None else jnp.maximum(best, cand)
        found = jnp.zeros((1, tm), jnp.bool_)
        for g in range(N_GROUPS):
            pick = jnp.logical_and(jnp.logical_and(gscore[g] == best, ~taken[g]), ~found)
            found = jnp.logical_or(found, pick)
            taken[g] = jnp.logical_or(taken[g], pick)
    masked = [jnp.where(taken[g], choice[g], neg) for g in range(N_GROUPS)]
    eidx = [sub + g * GROUP_SIZE for g in range(N_GROUPS)]
    out_row = lax.broadcasted_iota(I32, (SUBLANES, tm), 0)
    idx_out = jnp.zeros((SUBLANES, tm), I32)
    wgt_out = jnp.zeros((SUBLANES, tm), F32)
    wsum = jnp.zeros((1, tm), F32)
    for k in range(TOP_K):
        mx = masked[0]
        for g in range(1, N_GROUPS):
            mx = jnp.maximum(mx, masked[g])
        mx = colmax(mx)
        cand = jnp.where(masked[0] == mx, eidx[0], N_EXPERTS)
        for g in range(1, N_GROUPS):
            cand = jnp.minimum(cand, jnp.where(masked[g] == mx, eidx[g], N_EXPERTS))
        ik = jnp.min(cand, axis=0, keepdims=True)
        wk = jnp.where(eidx[0] == ik, scores[0], 0.0)
        for g in range(1, N_GROUPS):
            wk = wk + jnp.where(eidx[g] == ik, scores[g], 0.0)
        wk = jnp.sum(wk, axis=0, keepdims=True)
        masked = [jnp.where(eidx[g] == ik, neg, masked[g]) for g in range(N_GROUPS)]
        wsum = wsum + wk
        idx_out = jnp.where(out_row == k, ik, idx_out)
        wgt_out = jnp.where(out_row == k, wk, wgt_out)
    idx_ref[...] = idx_out
    wgt_ref[...] = wgt_out / wsum * ROUTE_SCALE


def _router(xb, w, bias):
    t = xb.shape[0]
    tm = _row_tile(t, 512)
    bias_b = jnp.broadcast_to(bias.astype(F32)[:, None], (N_EXPERTS, tm))
    return pl.pallas_call(
        _router_kernel,
        grid=(t // tm,),
        in_specs=[pl.BlockSpec((tm, D_MODEL), lambda i: (i, 0)),
                  pl.BlockSpec((N_EXPERTS, D_MODEL), lambda i: (0, 0)),
                  pl.BlockSpec((N_EXPERTS, tm), lambda i: (0, 0))],
        out_specs=[pl.BlockSpec((SUBLANES, tm), lambda i: (0, i)),
                   pl.BlockSpec((SUBLANES, tm), lambda i: (0, i))],
        out_shape=[jax.ShapeDtypeStruct((SUBLANES, t), I32), jax.ShapeDtypeStruct((SUBLANES, t), F32)],
        compiler_params=_cparams(("parallel",)),
        name="moe_router",
    )(xb, w.T, bias_b)


def _expert_kernel(blk_e_ref, nvalid_ref, tok_ref, tokn_ref, slot_ref, x_hbm, w1_ref, w3_ref, w2_ref,
                   o_hbm, xbuf, ybuf, w1b, w3b, w2b, gsem, ssem, *, rows):
    b = pl.program_id(0)
    nb = pl.num_programs(0)
    cur = b % 2
    nxt = 1 - cur

    def start_gather(tref, buf):
        for j in range(rows):
            pltpu.make_async_copy(x_hbm.at[pl.ds(tref[0, 0, j], 1), :], xbuf.at[buf, pl.ds(j, 1), :],
                                  gsem.at[buf]).start(priority=j % 2)

    def wait_gather(buf):
        pltpu.make_async_copy(x_hbm.at[pl.ds(0, rows), :], xbuf.at[buf], gsem.at[buf]).wait()

    def scatter_row(j, slot, buf):
        return pltpu.make_async_copy(ybuf.at[buf, pl.ds(j, 1), :], o_hbm.at[pl.ds(slot, 1), :], ssem.at[buf])

    def start_scatter(count, buf):
        def body(j, carry):
            scatter_row(j, slot_ref[0, 0, j], buf).start()
            return carry

        @pl.when(count == rows)
        def _():
            for j in range(rows):
                scatter_row(j, slot_ref[0, 0, j], buf).start(priority=j % 2)

        @pl.when(count < rows)
        def _():
            lax.fori_loop(0, count, body, 0)

    def wait_scatter(count, buf):
        @pl.when(count == rows)
        def _():
            pltpu.make_async_copy(ybuf.at[buf], o_hbm.at[pl.ds(0, rows), :], ssem.at[buf]).wait()

        @pl.when(count < rows)
        def _():
            def body(j, carry):
                scatter_row(0, 0, buf).wait()
                return carry
            lax.fori_loop(0, count, body, 0)

    @pl.when(b == 0)
    def _():
        start_gather(tok_ref, cur)

    @pl.when(b + 1 < nb)
    def _():
        start_gather(tokn_ref, nxt)

    wait_gather(cur)

    @pl.when(b >= 2)
    def _():
        wait_scatter(nvalid_ref[jnp.maximum(b - 2, 0)], cur)

    @pl.when(jnp.logical_or(b == 0, blk_e_ref[b] != blk_e_ref[jnp.maximum(b - 1, 0)]))
    def _():
        w1b[...] = w1_ref[0].astype(BF16)
        w3b[...] = w3_ref[0].astype(BF16)
        w2b[...] = w2_ref[0].astype(BF16)

    nv = nvalid_ref[b]

    @pl.when(nv > 0)
    def _():
        x = xbuf[cur].astype(BF16)
        h = jax.nn.silu(_dot(x, w1b[...])) * _dot(x, w3b[...])
        ybuf[cur] = _dot(h.astype(BF16), w2b[...])
        start_scatter(nv, cur)

    @pl.when(b == nb - 1)
    def _():
        wait_scatter(nv, cur)

        @pl.when(b >= 1)
        def _():
            wait_scatter(nvalid_ref[jnp.maximum(b - 1, 0)], nxt)


def _experts(x, w1, w3, w2, blk_e, nvalid, buf_tok, buf_slot, *, n_out_rows):
    rows = EXPERT_ROWS
    n_blocks = blk_e.shape[0]
    tok3 = buf_tok.reshape(n_blocks, 1, rows)
    slot3 = buf_slot.reshape(n_blocks, 1, rows)
    smem_blk = lambda f: pl.BlockSpec((1, 1, rows), f, memory_space=pltpu.SMEM)
    grid_spec = pltpu.PrefetchScalarGridSpec(
        num_scalar_prefetch=2,
        grid=(n_blocks,),
        in_specs=[
            smem_blk(lambda b, be, nv: (b, 0, 0)),
            smem_blk(lambda b, be, nv: (jnp.minimum(b + 1, n_blocks - 1), 0, 0)),
            smem_blk(lambda b, be, nv: (b, 0, 0)),
            pl.BlockSpec(memory_space=pl.ANY),
            pl.BlockSpec((1, D_MODEL, EXPERT_FF), lambda b, be, nv: (be[b], 0, 0)),
            pl.BlockSpec((1, D_MODEL, EXPERT_FF), lambda b, be, nv: (be[b], 0, 0)),
            pl.BlockSpec((1, EXPERT_FF, D_MODEL), lambda b, be, nv: (be[b], 0, 0)),
        ],
        out_specs=pl.BlockSpec(memory_space=pl.ANY),
        scratch_shapes=[pltpu.VMEM((2, rows, D_MODEL), F32), pltpu.VMEM((2, rows, D_MODEL), F32),
                        pltpu.VMEM((D_MODEL, EXPERT_FF), BF16), pltpu.VMEM((D_MODEL, EXPERT_FF), BF16),
                        pltpu.VMEM((EXPERT_FF, D_MODEL), BF16),
                        pltpu.SemaphoreType.DMA((2,)), pltpu.SemaphoreType.DMA((2,))],
    )
    return pl.pallas_call(
        functools.partial(_expert_kernel, rows=rows),
        grid_spec=grid_spec,
        out_shape=jax.ShapeDtypeStruct((n_out_rows, D_MODEL), F32),
        compiler_params=_cparams(("arbitrary",)),
        name="moe_experts",
    )(blk_e, nvalid, tok3, tok3, slot3, x, w1, w3, w2)


def _combine_kernel(x_ref, xb_ref, wt_ref, y0, y1, y2, y3, y4, y5, s1_ref, s3_ref, s2_ref, g_ref, b_ref,
                    o_ref, ob_ref):
    wt = wt_ref[...]
    routed = None
    for k, yk in enumerate((y0, y1, y2, y3, y4, y5)):
        term = yk[...] * wt[:, k:k + 1]
        routed = term if routed is None else routed + term
    xb = xb_ref[...]
    hs = jax.nn.silu(_dot(xb, s1_ref[...])) * _dot(xb, s3_ref[...])
    shared = _dot(hs.astype(BF16), s2_ref[...])
    y = _ln_rows(ALPHA * x_ref[...] + (routed + shared), g_ref[...], b_ref[...])
    o_ref[...] = y
    ob_ref[...] = y.astype(BF16)


def _combine(x, xb, wt, y6, s1, s3, s2, g, b):
    t = x.shape[0]
    tm = _row_tile(t, 128)
    nt = t // tm
    c2 = lambda i: (0, 0)
    yspecs = [pl.BlockSpec((tm, D_MODEL), (lambda i, k=k: (k * nt + i, 0))) for k in range(TOP_K)]
    return pl.pallas_call(
        _combine_kernel,
        grid=(nt,),
        in_specs=[
            pl.BlockSpec((tm, D_MODEL), lambda i: (i, 0)),
            pl.BlockSpec((tm, D_MODEL), lambda i: (i, 0)),
            pl.BlockSpec((tm, SUBLANES), lambda i: (i, 0)),
            *yspecs,
            pl.BlockSpec((D_MODEL, EXPERT_FF), c2),
            pl.BlockSpec((D_MODEL, EXPERT_FF), c2),
            pl.BlockSpec((EXPERT_FF, D_MODEL), c2),
            pl.BlockSpec((1, D_MODEL), c2),
            pl.BlockSpec((1, D_MODEL), c2),
        ],
        out_specs=[pl.BlockSpec((tm, D_MODEL), lambda i: (i, 0)),
                   pl.BlockSpec((tm, D_MODEL), lambda i: (i, 0))],
        out_shape=[jax.ShapeDtypeStruct((t, D_MODEL), F32), jax.ShapeDtypeStruct((t, D_MODEL), BF16)],
        compiler_params=_cparams(("parallel",)),
        name="moe_combine_ln2",
    )(x, xb, wt, y6, y6, y6, y6, y6, y6, s1, s3, s2, g.reshape(1, -1), b.reshape(1, -1))


def _route_plan(idx, t):
    rows = EXPERT_ROWS
    tk = t * TOP_K
    n_blocks = -(-(tk + N_EXPERTS * (rows - 1)) // rows)
    n_rows = n_blocks * rows
    flat_e = idx.reshape(-1)
    order = jnp.argsort(flat_e).astype(I32)
    counts = jnp.zeros((N_EXPERTS,), I32).at[flat_e].add(1)
    padded = (counts + rows - 1) // rows * rows
    pad_end = jnp.cumsum(padded)
    pad_start = pad_end - padded
    start = jnp.cumsum(counts) - counts
    blk = jnp.arange(n_blocks, dtype=I32)
    blk_e = jnp.minimum(jnp.sum(blk[:, None] * rows >= pad_end[None, :], axis=1), N_EXPERTS - 1).astype(I32)
    in_blk = jnp.arange(rows, dtype=I32)[None, :]
    j = blk[:, None] * rows + in_blk - pad_start[blk_e][:, None]
    valid = jnp.logical_and(j < counts[blk_e][:, None], (blk * rows < pad_end[N_EXPERTS - 1])[:, None])
    src = jnp.clip(start[blk_e][:, None] + j, 0, tk - 1)
    flat = order[src]
    tok = flat // TOP_K
    kk = flat - tok * TOP_K
    buf_tok = jnp.where(valid, tok, 0).astype(I32).reshape(-1)
    buf_slot = jnp.where(valid, kk * t + tok, 0).astype(I32).reshape(-1)
    nvalid = jnp.sum(valid, axis=1).astype(I32)
    return blk_e, nvalid, buf_tok, buf_slot


def _prep_layer(l, p):
    f32 = F32
    w_in = p["w_in"][l]
    offs = np.cumsum((0, 1024, 1024, 512, 512, 64, 1024))
    xa, ga, cq, ckv, kr, uc, gates = (w_in[:, offs[0]:offs[1]], w_in[:, offs[1]:offs[2]], w_in[:, offs[2]:offs[3]],
                                      w_in[:, offs[3]:offs[4]], w_in[:, offs[4]:offs[5]], w_in[:, offs[5]:offs[6]],
                                      w_in[:, offs[6]:])
    half = QK_ROPE // 2
    kr_sw = jnp.concatenate([kr[:, half:], kr[:, :half]], axis=1)
    w_in_aug = jnp.concatenate([xa, ga, cq, ckv, uc, gates, kr, kr_sw], axis=1).astype(BF16)

    wuq = p["mla_w_uq"][l]
    rope_w = wuq[:, :, QK_NOPE:]
    wuq_aug = jnp.concatenate([wuq, rope_w[:, :, half:], rope_w[:, :, :half]], axis=2)
    wuq_aug = jnp.transpose(wuq_aug, (1, 0, 2)).astype(BF16)
    wuk = jnp.transpose(p["mla_w_uk"][l], (1, 0, 2)).astype(BF16)
    wuv = jnp.transpose(p["mla_w_uv"][l], (1, 0, 2)).astype(BF16)

    wg = jnp.concatenate([p["lru_gate_a_w"][l], p["lru_gate_x_w"][l]], axis=-1).astype(BF16)
    bg = jnp.concatenate([p["lru_gate_a_b"][l], p["lru_gate_x_b"][l]], axis=-1)[:, None, :]
    sp = jax.nn.softplus(-p["lru_lambda"][l].astype(f32))

    dt = jnp.exp(p["s5_log_dt"][l].astype(f32))[:, None]
    lam_re = p["s5_lam_re"][l].astype(f32)
    lam_im = p["s5_lam_im"][l].astype(f32)
    mag = jnp.exp(lam_re * dt)
    a_re = mag * jnp.cos(lam_im * dt)
    a_im = mag * jnp.sin(lam_im * dt)
    den = jnp.square(lam_re) + jnp.square(lam_im)
    f_re = ((a_re - 1.0) * lam_re + a_im * lam_im) / den
    f_im = (a_im * lam_re - (a_re - 1.0) * lam_im) / den
    b_re = p["s5_b_re"][l].astype(f32)
    b_im = p["s5_b_im"][l].astype(f32)
    bb_re = f_re[..., None] * b_re - f_im[..., None] * b_im
    bb_im = f_re[..., None] * b_im + f_im[..., None] * b_re
    gpr = S5_GROUPS // S5_RG
    eye = jnp.eye(gpr, dtype=f32)

    def in_bd(bb):
        bbr = bb.reshape(S5_RG, gpr, S5_STATE, S5_GROUP)
        return jnp.einsum("rgpc,gh->rgchp", bbr, eye).reshape(S5_RG, gpr * S5_GROUP, gpr * S5_STATE).astype(BF16)

    def out_bd(cm):
        cr = cm.astype(f32).reshape(S5_RG, gpr, S5_GROUP, S5_STATE)
        return jnp.einsum("rgcp,gh->rgphc", cr, eye).reshape(S5_RG, gpr * S5_STATE, gpr * S5_GROUP).astype(BF16)

    s5 = dict(bbr=in_bd(bb_re), bbi=in_bd(bb_im), ccr=out_bd(p["s5_c_re"][l]), cci=out_bd(p["s5_c_im"][l]),
              ar=a_re.reshape(1, -1), ai=a_im.reshape(1, -1), d=p["s5_d"][l].astype(f32).reshape(1, -1),
              glu_w=p["s5_glu_w"][l].astype(BF16), glu_b=p["s5_glu_b"][l].reshape(1, -1))
    return dict(
        w_in=w_in_aug, wuq=wuq_aug, wuk=wuk, wuv=wuv, wg=wg, bg=bg, sp=sp, s5=s5,
        wa=p["w_branch_a"][l].astype(BF16), wb=p["w_branch_b"][l].astype(BF16), wc=p["w_branch_c"][l].astype(BF16),
        w_out=p["w_out"][l].astype(BF16), router_w=p["router_w"][l].astype(BF16),
        w1=p["exp_w1"][l], w3=p["exp_w3"][l], w2=p["exp_w2"][l],
        s1=p["sh_w1"][l].astype(BF16), s3=p["sh_w3"][l].astype(BF16), s2=p["sh_w2"][l].astype(BF16),
    )


def _rope_tables(bp, sp_len, bs, ss_len, past):
    half = QK_ROPE // 2
    inv = ROPE_THETA ** (-jnp.arange(half, dtype=F32) / half)
    pos = jnp.concatenate([jnp.tile(jnp.arange(sp_len, dtype=I32), bp),
                           jnp.tile(past + jnp.arange(ss_len, dtype=I32), bs)])
    ang = pos.astype(F32)[:, None] * inv
    cos, sin = jnp.cos(ang), jnp.sin(ang)
    return jnp.concatenate([cos, cos], axis=1), jnp.concatenate([-sin, sin], axis=1)


def kernel(x_prompt, x_sample, cache_mla_ckv, cache_mla_krope, state_lru_conv, state_lru_h, state_s5_re, state_s5_im, ln_in_g, ln_in_b, w_in, lru_conv_w, lru_conv_b, lru_gate_a_w, lru_gate_a_b, lru_gate_x_w, lru_gate_x_b, lru_lambda, mla_q_norm_g, mla_w_uq, mla_kv_norm_g, mla_w_uk, mla_w_uv, s5_lam_re, s5_lam_im, s5_log_dt, s5_b_re, s5_b_im, s5_c_re, s5_c_im, s5_d, s5_glu_w, s5_glu_b, w_branch_a, w_branch_b, w_branch_c, w_out, b_out, ln1_g, ln1_b, router_w, router_bias, exp_w1, exp_w3, exp_w2, sh_w1, sh_w3, sh_w2, ln2_g, ln2_b):
    p = dict(w_in=w_in, lru_gate_a_w=lru_gate_a_w, lru_gate_a_b=lru_gate_a_b, lru_gate_x_w=lru_gate_x_w,
             lru_gate_x_b=lru_gate_x_b, lru_lambda=lru_lambda, mla_w_uq=mla_w_uq, mla_w_uk=mla_w_uk,
             mla_w_uv=mla_w_uv, s5_lam_re=s5_lam_re, s5_lam_im=s5_lam_im, s5_log_dt=s5_log_dt, s5_b_re=s5_b_re,
             s5_b_im=s5_b_im, s5_c_re=s5_c_re, s5_c_im=s5_c_im, s5_d=s5_d, s5_glu_w=s5_glu_w, s5_glu_b=s5_glu_b,
             w_branch_a=w_branch_a, w_branch_b=w_branch_b, w_branch_c=w_branch_c, w_out=w_out, router_w=router_w,
             exp_w1=exp_w1, exp_w3=exp_w3, exp_w2=exp_w2, sh_w1=sh_w1, sh_w3=sh_w3, sh_w2=sh_w2)
    bp, sp_len, d = x_prompt.shape
    bs, ss_len, _ = x_sample.shape
    past = cache_mla_ckv.shape[2]
    depth = w_in.shape[0]
    assert d == D_MODEL and sp_len % CHUNK == 0 and ss_len == CHUNK
    tp, ts = bp * sp_len, bs * ss_len
    t = tp + ts

    cos2, sin2 = _rope_tables(bp, sp_len, bs, ss_len, past)
    x, xb = _ln_in(x_prompt.reshape(tp, d), x_sample.reshape(ts, d), ln_in_g, ln_in_b)

    new_p, new_s = [], []
    for l in range(depth):
        lp = _prep_layer(l, p)
        proj = _in_proj(xb, lp["w_in"])

        conv0 = jnp.concatenate([jnp.zeros((bp, CONV_WIDTH - 1, LRU_WIDTH), F32), state_lru_conv[l]], axis=0)
        h0 = jnp.concatenate([jnp.zeros((bp, LRU_WIDTH), F32), state_lru_h[l]], axis=0)[:, None, :]
        y_a, conv_new, h_new = _lru_branch(proj, conv0, h0, lru_conv_w[l], lru_conv_b[l], lp["wg"], lp["bg"],
                                           lp["sp"], bp=bp, sp_len=sp_len, bs=bs)
        h_new = h_new[:, 0, :]

        q = _q_proj(proj, mla_q_norm_g[l], lp["wuq"], cos2, sin2)
        ckv_new, kr_new, k, v = _kv_proj(proj, mla_kv_norm_g[l], lp["wuk"], lp["wuv"], cos2, sin2)
        kpast, vpast = _kv_cache_proj(cache_mla_ckv[l].reshape(bs * past, KV_LORA),
                                      cache_mla_krope[l].reshape(bs * past, QK_ROPE), lp["wuk"], lp["wuv"])
        yb_p = _attn_prompt(q, k, v, bp=bp, sp_len=sp_len)
        yb_s = _attn_sample(q, k, v, kpast, vpast, bs=bs, ss_len=ss_len, past=past, tp=tp)

        zero_state = jnp.zeros((bp, S5_HID), F32)
        yc_p, s5r_p, s5i_p = _s5_branch(proj, 0, zero_state, zero_state, lp["s5"], nb=bp, seq_len=sp_len)
        yc_s, s5r_s, s5i_s = _s5_branch(proj, tp, state_s5_re[l].reshape(bs, S5_HID),
                                        state_s5_im[l].reshape(bs, S5_HID), lp["s5"], nb=bs, seq_len=ss_len)
        yc_p = yc_p.reshape(tp, S5_WIDTH)
        yc_s = yc_s.reshape(ts, S5_WIDTH)

        merged = _merge(y_a, yb_p, yb_s, yc_p, yc_s, proj, lp["wa"], lp["wb"], lp["wc"])
        x, xb = _out_ln(x, merged, lp["w_out"], b_out[l], ln1_g[l], ln1_b[l])

        idx8, wt8 = _router(xb, lp["router_w"], router_bias[l])
        blk_e, nvalid, buf_tok, buf_slot = _route_plan(idx8[:TOP_K].T, t)
        y6 = _experts(x, lp["w1"], lp["w3"], lp["w2"], blk_e, nvalid, buf_tok, buf_slot, n_out_rows=TOP_K * t)
        x, xb = _combine(x, xb, wt8.T, y6, lp["s1"], lp["s3"], lp["s2"], ln2_g[l], ln2_b[l])

        new_p.append((ckv_new[:tp].reshape(bp, sp_len, KV_LORA), kr_new[:tp].reshape(bp, sp_len, QK_ROPE),
                      conv_new[:bp], h_new[:bp], s5r_p.reshape(bp, S5_GROUPS, S5_STATE),
                      s5i_p.reshape(bp, S5_GROUPS, S5_STATE)))
        new_s.append((ckv_new[tp:].reshape(bs, ss_len, KV_LORA), kr_new[tp:].reshape(bs, ss_len, QK_ROPE),
                      conv_new[bp:], h_new[bp:], s5r_s.reshape(bs, S5_GROUPS, S5_STATE),
                      s5i_s.reshape(bs, S5_GROUPS, S5_STATE)))

    p_out = [jnp.stack(z) for z in zip(*new_p)]
    s_out = [jnp.stack(z) for z in zip(*new_s)]
    return (x[:tp].reshape(bp, sp_len, d), x[tp:].reshape(bs, ss_len, d), *p_out, *s_out)
```

```python
import functools
import math

import jax
import jax.numpy as jnp
import numpy as np
from jax import lax
from jax.experimental import pallas as pl
from jax.experimental.pallas import tpu as pltpu

F32 = jnp.float32
BF16 = jnp.bfloat16
I32 = jnp.int32

D_MODEL = 2048
DEPTH = 2
CHUNK = 64
LN_EPS = 1e-5
RMS_EPS = 1e-6
ALPHA = (2 * DEPTH) ** 0.25
LRU_WIDTH = 1024
LRU_BLOCKS = 8
LRU_BLOCK = 128
CONV_WIDTH = 4
LRU_C = 8.0
MLA_HEADS = 16
QK_NOPE = 128
QK_ROPE = 64
QK_DIM = QK_NOPE + QK_ROPE
V_HEAD = 128
Q_LORA = 512
KV_LORA = 512
ROPE_THETA = 10000.0
S5_WIDTH = 1024
S5_GROUP = 16
S5_GROUPS = 64
S5_STATE = 64
S5_HID = S5_GROUPS * S5_STATE
N_EXPERTS = 64
TOP_K = 6
N_GROUPS = 8
GROUP_SIZE = N_EXPERTS // N_GROUPS
TOPK_GROUPS = 4
EXPERT_FF = 512
ROUTE_SCALE = 2.5

COL_XA, COL_GA, COL_CQ, COL_CKV, COL_UC, COL_GATES, COL_KR = 0, 1024, 2048, 2560, 3072, 4096, 10240
N_PROJ = 10368

V7X_VMEM_BYTES = 64 * 1024 * 1024
VMEM_LIMIT = V7X_VMEM_BYTES - 8 * 1024 * 1024
LANES = 128
SUBLANES = 8

EXPERT_ROWS = 256
ATTN_TQ = 1024
ATTN_HEADS_PER_STEP = 2


def _cparams(sem):
    return pltpu.CompilerParams(dimension_semantics=sem, vmem_limit_bytes=VMEM_LIMIT)


def _row_tile(n, pref):
    for t in (1024, 512, 256, 128, 64, 32, 16, 8):
        if t <= pref and n % t == 0:
            return t
    raise ValueError(f"no row tile for {n}")


def _ln_rows(x, g, b):
    mu = jnp.mean(x, axis=-1, keepdims=True)
    xc = x - mu
    var = jnp.mean(xc * xc, axis=-1, keepdims=True)
    return xc * lax.rsqrt(var + LN_EPS) * g + b


def _dot(a, b):
    return jnp.dot(a, b, preferred_element_type=F32)


def _ln_in_kernel(xp_ref, xs_ref, g_ref, b_ref, o_ref, ob_ref, *, n_prompt_tiles):
    i = pl.program_id(0)

    def emit(x):
        y = _ln_rows(x, g_ref[...], b_ref[...])
        o_ref[...] = y
        ob_ref[...] = y.astype(BF16)

    @pl.when(i < n_prompt_tiles)
    def _():
        emit(xp_ref[...])

    @pl.when(i >= n_prompt_tiles)
    def _():
        emit(xs_ref[...])


def _ln_in(xp, xs, g, b):
    tp, ts = xp.shape[0], xs.shape[0]
    tm = _row_tile(math.gcd(tp, ts), 512)
    npt, nst = tp // tm, ts // tm
    t = tp + ts
    return pl.pallas_call(
        functools.partial(_ln_in_kernel, n_prompt_tiles=npt),
        grid=(npt + nst,),
        in_specs=[
            pl.BlockSpec((tm, D_MODEL), lambda i: (jnp.minimum(i, npt - 1), 0)),
            pl.BlockSpec((tm, D_MODEL), lambda i: (jnp.maximum(i - npt, 0), 0)),
            pl.BlockSpec((1, D_MODEL), lambda i: (0, 0)),
            pl.BlockSpec((1, D_MODEL), lambda i: (0, 0)),
        ],
        out_specs=[pl.BlockSpec((tm, D_MODEL), lambda i: (i, 0)),
                   pl.BlockSpec((tm, D_MODEL), lambda i: (i, 0))],
        out_shape=[jax.ShapeDtypeStruct((t, D_MODEL), F32), jax.ShapeDtypeStruct((t, D_MODEL), BF16)],
        compiler_params=_cparams(("parallel",)),
        name="ln_in",
    )(xp, xs, g.reshape(1, -1), b.reshape(1, -1))


def _mm_kernel(x_ref, w_ref, o_ref):
    o_ref[...] = _dot(x_ref[...], w_ref[...])


def _in_proj(xb, w):
    t, k = xb.shape
    n = w.shape[1]
    tm = _row_tile(t, 512)
    tn = 1152
    assert n % tn == 0
    return pl.pallas_call(
        _mm_kernel,
        grid=(n // tn, t // tm),
        in_specs=[pl.BlockSpec((tm, k), lambda j, i: (i, 0)),
                  pl.BlockSpec((k, tn), lambda j, i: (0, j))],
        out_specs=pl.BlockSpec((tm, tn), lambda j, i: (i, j)),
        out_shape=jax.ShapeDtypeStruct((t, n), F32),
        compiler_params=_cparams(("parallel", "parallel")),
        name="in_proj",
    )(xb, w)


def _lru_kernel(xa_ref, ga_ref, cs_ref, h0_ref, cw_ref, cb_ref, wg_ref, bg_ref, sp_ref,
                y_ref, cn_ref, hn_ref, xbuf, hcar, *, rows, n_prompt_tiles, tiles_per_seq):
    c = pl.program_id(0)
    first = jnp.logical_or(c >= n_prompt_tiles, c % tiles_per_seq == 0)

    @pl.when(first)
    def _():
        xbuf[5:8, :] = cs_ref[0]
        hcar[0:1, :] = h0_ref[0]

    xa = xa_ref[...]
    xbuf[8:8 + rows, :] = xa
    cw = cw_ref[...]
    xc = (cb_ref[...] + cw[0:1] * xbuf[5:5 + rows, :] + cw[1:2] * xbuf[6:6 + rows, :]
          + cw[2:3] * xbuf[7:7 + rows, :] + cw[3:4] * xa)
    tail = xa[rows - 3:rows, :]
    xbuf[5:8, :] = tail
    cn_ref[0] = tail

    row = lax.broadcasted_iota(I32, (rows, LRU_BLOCK), 0)
    for n in range(LRU_BLOCKS):
        sl = slice(n * LRU_BLOCK, (n + 1) * LRU_BLOCK)
        xcb = xc[:, sl]
        g = _dot(xcb.astype(BF16), wg_ref[n]) + bg_ref[n]
        r = jax.nn.sigmoid(g[:, :LRU_BLOCK])
        gi = jax.nn.sigmoid(g[:, LRU_BLOCK:])
        log_a = -LRU_C * r * sp_ref[:, sl]
        a = jnp.exp(log_a)
        th = jnp.tanh(log_a)
        b = jnp.sqrt(-2.0 * th / (1.0 - th)) * (gi * xcb)
        s = 1
        while s < rows:
            keep = row >= s
            a_s = jnp.where(keep, pltpu.roll(a, s, 0), 1.0)
            b_s = jnp.where(keep, pltpu.roll(b, s, 0), 0.0)
            b = a * b_s + b
            a = a * a_s
            s *= 2
        h = a * hcar[0:1, sl] + b
        hcar[0:1, sl] = h[rows - 1:rows, :]
        y_ref[:, sl] = (h * jax.nn.gelu(ga_ref[:, sl])).astype(BF16)
    hn_ref[0] = hcar[0:1, :]


def _lru_branch(proj, conv_state, h_state, conv_w, conv_b, wg, bg, sp, *, bp, sp_len, bs):
    t = proj.shape[0]
    rows = CHUNK
    tps = sp_len // rows
    npt = bp * tps
    nt = t // rows
    nseq = bp + bs
    w = LRU_WIDTH

    def seq_of(c):
        return jnp.where(c < npt, c // tps, bp + (c - npt))

    return pl.pallas_call(
        functools.partial(_lru_kernel, rows=rows, n_prompt_tiles=npt, tiles_per_seq=tps),
        grid=(nt,),
        in_specs=[
            pl.BlockSpec((rows, w), lambda c: (c, COL_XA // w)),
            pl.BlockSpec((rows, w), lambda c: (c, COL_GA // w)),
            pl.BlockSpec((1, 3, w), lambda c: (seq_of(c), 0, 0)),
            pl.BlockSpec((1, 1, w), lambda c: (seq_of(c), 0, 0)),
            pl.BlockSpec((CONV_WIDTH, w), lambda c: (0, 0)),
            pl.BlockSpec((1, w), lambda c: (0, 0)),
            pl.BlockSpec((LRU_BLOCKS, LRU_BLOCK, 2 * LRU_BLOCK), lambda c: (0, 0, 0)),
            pl.BlockSpec((LRU_BLOCKS, 1, 2 * LRU_BLOCK), lambda c: (0, 0, 0)),
            pl.BlockSpec((1, w), lambda c: (0, 0)),
        ],
        out_specs=[
            pl.BlockSpec((rows, w), lambda c: (c, 0)),
            pl.BlockSpec((1, 3, w), lambda c: (seq_of(c), 0, 0)),
            pl.BlockSpec((1, 1, w), lambda c: (seq_of(c), 0, 0)),
        ],
        out_shape=[jax.ShapeDtypeStruct((t, w), BF16),
                   jax.ShapeDtypeStruct((nseq, 3, w), F32),
                   jax.ShapeDtypeStruct((nseq, 1, w), F32)],
        scratch_shapes=[pltpu.VMEM((rows + 8, w), F32), pltpu.VMEM((SUBLANES, w), F32)],
        compiler_params=_cparams(("arbitrary",)),
        name="rglru",
    )(proj, proj, conv_state, h_state, conv_w, conv_b.reshape(1, -1), wg, bg, sp.reshape(1, -1))


S5_COLS = 512
S5_RG = 4


def _s5_kernel(*refs, nb, steps):
    u_refs = refs[:nb]
    (h0r_ref, h0i_ref, bbr_ref, bbi_ref, ccr_ref, cci_ref, ar_ref, ai_ref, d_ref, gw_ref, gb_ref,
     o_ref, hr_out, hi_out, bur, bui, hr, hi, tm_in, tm_out) = refs[nb:]
    i = pl.program_id(0)
    n_slab = S5_WIDTH // LANES

    @pl.when(i == 0)
    def _():
        hr[...] = h0r_ref[...]
        hi[...] = h0i_ref[...]

    for b in range(nb):
        ub = u_refs[b][...]
        for s in range(n_slab):
            tm_in[s, pl.ds(b, steps, stride=nb), :] = ub[:, s * LANES:(s + 1) * LANES]
    u = jnp.concatenate([tm_in[s] for s in range(n_slab)], axis=-1)
    ub = u.astype(BF16)
    kin = S5_WIDTH // S5_RG
    kst = S5_HID // S5_RG
    for r in range(S5_RG):
        ur = ub[:, r * kin:(r + 1) * kin]
        bur[:, r * kst:(r + 1) * kst] = _dot(ur, bbr_ref[r])
        bui[:, r * kst:(r + 1) * kst] = _dot(ur, bbi_ref[r])

    for cc in range(S5_HID // S5_COLS):
        cs = slice(cc * S5_COLS, (cc + 1) * S5_COLS)
        ar = jnp.broadcast_to(ar_ref[:, cs], (SUBLANES, S5_COLS))
        ai = jnp.broadcast_to(ai_ref[:, cs], (SUBLANES, S5_COLS))
        for bg in range(nb // SUBLANES):
            bsl = slice(bg * SUBLANES, (bg + 1) * SUBLANES)

            def body(t, carry, cs=cs, bg=bg, ar=ar, ai=ai):
                h_r, h_i = carry
                rsl = pl.ds(pl.multiple_of(t * nb + bg * SUBLANES, SUBLANES), SUBLANES)
                n_r = ar * h_r - ai * h_i + bur[rsl, cs]
                n_i = ar * h_i + ai * h_r + bui[rsl, cs]
                bur[rsl, cs] = n_r
                bui[rsl, cs] = n_i
                return n_r, n_i

            h_r, h_i = lax.fori_loop(0, steps, body, (hr[bsl, cs], hi[bsl, cs]))
            hr[bsl, cs] = h_r
            hi[bsl, cs] = h_i

    ys = []
    for r in range(S5_RG):
        hrb = bur[:, r * kst:(r + 1) * kst].astype(BF16)
        hib = bui[:, r * kst:(r + 1) * kst].astype(BF16)
        ys.append(_dot(hrb, ccr_ref[r]) - _dot(hib, cci_ref[r]))
    y = jnp.concatenate(ys, axis=-1) + d_ref[...] * u
    z = jax.nn.gelu(y)
    gate = jax.nn.sigmoid(_dot(z.astype(BF16), gw_ref[...]) + gb_ref[...])
    out = z * gate
    for s in range(n_slab):
        tm_out[s] = out[:, s * LANES:(s + 1) * LANES]
    for b in range(nb):
        ob = jnp.concatenate([tm_out[s, pl.ds(b, steps, stride=nb), :] for s in range(n_slab)], axis=-1)
        o_ref[b] = ob.astype(BF16)
    hr_out[...] = hr[...]
    hi_out[...] = hi[...]


def _s5_branch(proj, row_off, h0r, h0i, pw, *, nb, seq_len):
    assert nb % SUBLANES == 0
    steps = max(16, 256 // nb)
    assert seq_len % steps == 0 and row_off % steps == 0
    rows = steps * nb
    kin = S5_WIDTH // S5_RG
    kst = S5_HID // S5_RG
    n_slab = S5_WIDTH // LANES
    const2 = lambda i: (0, 0)
    const3 = lambda i: (0, 0, 0)
    tiles_per_seq = seq_len // steps
    u_specs = [pl.BlockSpec((steps, S5_WIDTH),
                            (lambda i, b=b: (row_off // steps + b * tiles_per_seq + i, COL_UC // S5_WIDTH)))
               for b in range(nb)]
    return pl.pallas_call(
        functools.partial(_s5_kernel, nb=nb, steps=steps),
        grid=(tiles_per_seq,),
        in_specs=[
            *u_specs,
            pl.BlockSpec((nb, S5_HID), const2),
            pl.BlockSpec((nb, S5_HID), const2),
            pl.BlockSpec((S5_RG, kin, kst), const3),
            pl.BlockSpec((S5_RG, kin, kst), const3),
            pl.BlockSpec((S5_RG, kst, kin), const3),
            pl.BlockSpec((S5_RG, kst, kin), const3),
            pl.BlockSpec((1, S5_HID), const2),
            pl.BlockSpec((1, S5_HID), const2),
            pl.BlockSpec((1, S5_WIDTH), const2),
            pl.BlockSpec((S5_WIDTH, S5_WIDTH), const2),
            pl.BlockSpec((1, S5_WIDTH), const2),
        ],
        out_specs=[pl.BlockSpec((nb, steps, S5_WIDTH), lambda i: (0, i, 0)),
                   pl.BlockSpec((nb, S5_HID), const2),
                   pl.BlockSpec((nb, S5_HID), const2)],
        out_shape=[jax.ShapeDtypeStruct((nb, seq_len, S5_WIDTH), BF16),
                   jax.ShapeDtypeStruct((nb, S5_HID), F32),
                   jax.ShapeDtypeStruct((nb, S5_HID), F32)],
        scratch_shapes=[pltpu.VMEM((rows, S5_HID), F32), pltpu.VMEM((rows, S5_HID), F32),
                        pltpu.VMEM((nb, S5_HID), F32), pltpu.VMEM((nb, S5_HID), F32),
                        pltpu.VMEM((n_slab, rows, LANES), F32), pltpu.VMEM((n_slab, rows, LANES), F32)],
        compiler_params=_cparams(("arbitrary",)),
        name="s5",
    )(*([proj] * nb), h0r, h0i, pw["bbr"], pw["bbi"], pw["ccr"], pw["cci"], pw["ar"], pw["ai"], pw["d"],
      pw["glu_w"], pw["glu_b"])


def _rms_rows(x, g):
    return x * lax.rsqrt(jnp.mean(x * x, axis=-1, keepdims=True) + RMS_EPS) * g


def _qproj_kernel(cq_ref, g_ref, w_ref, cos_ref, sin_ref, q_ref):
    xn = _rms_rows(cq_ref[...], g_ref[...]).astype(BF16)
    cos2 = cos_ref[...]
    sin2 = sin_ref[...]
    for h in range(MLA_HEADS):
        r = _dot(xn, w_ref[h])
        q_ref[h, :, 0:QK_NOPE] = r[:, 0:QK_NOPE].astype(BF16)
        rot = r[:, QK_NOPE:QK_DIM] * cos2 + r[:, QK_DIM:QK_DIM + QK_ROPE] * sin2
        q_ref[h, :, QK_NOPE:QK_DIM] = rot.astype(BF16)


def _q_proj(proj, g, w_aug, cos2, sin2):
    t = proj.shape[0]
    tm = _row_tile(t, 256)
    return pl.pallas_call(
        _qproj_kernel,
        grid=(t // tm,),
        in_specs=[
            pl.BlockSpec((tm, Q_LORA), lambda i: (i, COL_CQ // Q_LORA)),
            pl.BlockSpec((1, Q_LORA), lambda i: (0, 0)),
            pl.BlockSpec((MLA_HEADS, Q_LORA, 2 * LANES), lambda i: (0, 0, 0)),
            pl.BlockSpec((tm, QK_ROPE), lambda i: (i, 0)),
            pl.BlockSpec((tm, QK_ROPE), lambda i: (i, 0)),
        ],
        out_specs=pl.BlockSpec((MLA_HEADS, tm, QK_DIM), lambda i: (0, i, 0)),
        out_shape=jax.ShapeDtypeStruct((MLA_HEADS, t, QK_DIM), BF16),
        compiler_params=_cparams(("parallel",)),
        name="mla_q_proj",
    )(proj, g.reshape(1, -1), w_aug, cos2, sin2)


def _kvproj_kernel(ckv_ref, kr_ref, g_ref, wk_ref, wv_ref, cos_ref, sin_ref,
                   ckvn_ref, krn_ref, k_ref, v_ref):
    xn = _rms_rows(ckv_ref[...], g_ref[...])
    ckvn_ref[...] = xn
    xb = xn.astype(BF16)
    kr = kr_ref[...]
    rot = kr[:, 0:QK_ROPE] * cos_ref[...] + kr[:, QK_ROPE:2 * QK_ROPE] * sin_ref[...]
    krn_ref[...] = rot
    rb = rot.astype(BF16)
    for h in range(MLA_HEADS):
        k_ref[h, :, 0:QK_NOPE] = _dot(xb, wk_ref[h]).astype(BF16)
        k_ref[h, :, QK_NOPE:QK_DIM] = rb
        v_ref[h] = _dot(xb, wv_ref[h]).astype(BF16)


def _kv_proj(proj, g, wk, wv, cos2, sin2):
    t = proj.shape[0]
    tm = _row_tile(t, 256)
    c3 = lambda i: (0, 0, 0)
    return pl.pallas_call(
        _kvproj_kernel,
        grid=(t // tm,),
        in_specs=[
            pl.BlockSpec((tm, KV_LORA), lambda i: (i, COL_CKV // KV_LORA)),
            pl.BlockSpec((tm, LANES), lambda i: (i, COL_KR // LANES)),
            pl.BlockSpec((1, KV_LORA), lambda i: (0, 0)),
            pl.BlockSpec((MLA_HEADS, KV_LORA, QK_NOPE), c3),
            pl.BlockSpec((MLA_HEADS, KV_LORA, V_HEAD), c3),
            pl.BlockSpec((tm, QK_ROPE), lambda i: (i, 0)),
            pl.BlockSpec((tm, QK_ROPE), lambda i: (i, 0)),
        ],
        out_specs=[
            pl.BlockSpec((tm, KV_LORA), lambda i: (i, 0)),
            pl.BlockSpec((tm, QK_ROPE), lambda i: (i, 0)),
            pl.BlockSpec((MLA_HEADS, tm, QK_DIM), lambda i: (0, i, 0)),
            pl.BlockSpec((MLA_HEADS, tm, V_HEAD), lambda i: (0, i, 0)),
        ],
        out_shape=[jax.ShapeDtypeStruct((t, KV_LORA), F32),
                   jax.ShapeDtypeStruct((t, QK_ROPE), F32),
                   jax.ShapeDtypeStruct((MLA_HEADS, t, QK_DIM), BF16),
                   jax.ShapeDtypeStruct((MLA_HEADS, t, V_HEAD), BF16)],
        compiler_params=_cparams(("parallel",)),
        name="mla_kv_proj",
    )(proj, proj, g.reshape(1, -1), wk, wv, cos2, sin2)


def _qk(q, k):
    return lax.dot_general(q, k, (((1,), (1,)), ((), ())), preferred_element_type=F32)


def _attn_prompt_kernel(q_ref, k_ref, v_ref, o_ref, *, tq, scale):
    qi = pl.program_id(2)
    c = scale * math.log2(math.e)
    nh = q_ref.shape[0]
    qs = [q_ref[h] for h in range(nh)]

    def kv_rows(ki):
        return pl.ds(pl.multiple_of(ki * tq, tq), tq)

    def scores(h, ki):
        return _qk(qs[h], k_ref[h, kv_rows(ki), :])

    def update(state, s, v):
        m, l, acc = state
        m_new = jnp.maximum(m, jnp.max(s, axis=-1, keepdims=True))
        alpha = jnp.exp2((m - m_new) * c)
        p = jnp.exp2((s - m_new) * c)
        l = alpha * l + jnp.sum(p, axis=-1, keepdims=True)
        acc = alpha * acc + _dot(p.astype(BF16), v)
        return m_new, l, acc

    def body(ki, states):
        return tuple(update(states[h], scores(h, ki), v_ref[h, kv_rows(ki), :]) for h in range(nh))

    init = tuple((jnp.full((tq, 1), -jnp.inf, F32), jnp.zeros((tq, 1), F32), jnp.zeros((tq, V_HEAD), F32))
                 for _ in range(nh))
    states = lax.fori_loop(0, qi, body, init)
    rr = lax.broadcasted_iota(I32, (tq, tq), 0) // CHUNK
    cc = lax.broadcasted_iota(I32, (tq, tq), 1) // CHUNK
    for h in range(nh):
        s = jnp.where(cc <= rr, scores(h, qi), -jnp.inf)
        _, l, acc = update(states[h], s, v_ref[h, kv_rows(qi), :])
        o_ref[:, h * V_HEAD:(h + 1) * V_HEAD] = (acc / l).astype(BF16)


def _attn_prompt(q, k, v, *, bp, sp_len):
    tq = _row_tile(sp_len, ATTN_TQ)
    nq = sp_len // tq
    nh = ATTN_HEADS_PER_STEP
    scale = QK_DIM ** -0.5
    return pl.pallas_call(
        functools.partial(_attn_prompt_kernel, tq=tq, scale=scale),
        grid=(bp, MLA_HEADS // nh, nq),
        in_specs=[
            pl.BlockSpec((nh, tq, QK_DIM), lambda b, h, i: (h, b * nq + i, 0)),
            pl.BlockSpec((nh, sp_len, QK_DIM), lambda b, h, i: (h, b, 0)),
            pl.BlockSpec((nh, sp_len, V_HEAD), lambda b, h, i: (h, b, 0)),
        ],
        out_specs=pl.BlockSpec((tq, nh * V_HEAD), lambda b, h, i: (b * nq + i, h)),
        out_shape=jax.ShapeDtypeStruct((bp * sp_len, MLA_HEADS * V_HEAD), BF16),
        compiler_params=_cparams(("parallel", "parallel", "arbitrary")),
        name="mla_attn_prompt",
    )(q, k, v)


def _attn_sample_kernel(q_ref, cp_ref, rp_ref, cn_ref, rn_ref, wkt_ref, wv_ref, o_ref, *, scale):
    hh, sq = q_ref.shape[0], q_ref.shape[1]
    cp = cp_ref[0, 0].astype(BF16)
    cn = cn_ref[...].astype(BF16)
    rp = rp_ref[0, 0].astype(BF16)
    rn = rn_ref[...].astype(BF16)
    qa = jnp.concatenate([_dot(q_ref[h, :, 0:QK_NOPE], wkt_ref[h]).astype(BF16) for h in range(hh)], axis=0)
    qr = jnp.concatenate([q_ref[h, :, QK_NOPE:QK_DIM] for h in range(hh)], axis=0)
    s1 = (_qk(qa, cp) + _qk(qr, rp)) * scale
    s2 = (_qk(qa, cn) + _qk(qr, rn)) * scale
    m = jnp.maximum(jnp.max(s1, axis=-1, keepdims=True), jnp.max(s2, axis=-1, keepdims=True))
    p1 = jnp.exp(s1 - m)
    p2 = jnp.exp(s2 - m)
    l = jnp.sum(p1, axis=-1, keepdims=True) + jnp.sum(p2, axis=-1, keepdims=True)
    ol = (_dot(p1.astype(BF16), cp) + _dot(p2.astype(BF16), cn)) / l
    for h in range(hh):
        oh = _dot(ol[h * sq:(h + 1) * sq, :].astype(BF16), wv_ref[h])
        o_ref[:, h * V_HEAD:(h + 1) * V_HEAD] = oh.astype(BF16)


def _attn_sample(q, ckv_new, kr_new, cache_ckv, cache_kr, layer, wkt, wv, *, bs, ss_len, past, tp):
    assert ss_len == CHUNK and past % CHUNK == 0 and tp % ss_len == 0
    off = tp // ss_len
    scale = QK_DIM ** -0.5
    hh = MLA_HEADS
    c3 = lambda b: (0, 0, 0)
    return pl.pallas_call(
        functools.partial(_attn_sample_kernel, scale=scale),
        grid=(bs,),
        in_specs=[
            pl.BlockSpec((hh, ss_len, QK_DIM), lambda b: (0, off + b, 0)),
            pl.BlockSpec((1, 1, past, KV_LORA), lambda b: (layer, b, 0, 0)),
            pl.BlockSpec((1, 1, past, QK_ROPE), lambda b: (layer, b, 0, 0)),
            pl.BlockSpec((ss_len, KV_LORA), lambda b: (off + b, 0)),
            pl.BlockSpec((ss_len, QK_ROPE), lambda b: (off + b, 0)),
            pl.BlockSpec((hh, QK_NOPE, KV_LORA), c3),
            pl.BlockSpec((hh, KV_LORA, V_HEAD), c3),
        ],
        out_specs=pl.BlockSpec((ss_len, hh * V_HEAD), lambda b: (b, 0)),
        out_shape=jax.ShapeDtypeStruct((bs * ss_len, hh * V_HEAD), BF16),
        compiler_params=_cparams(("parallel",)),
        name="mla_attn_sample",
    )(q, cache_ckv, cache_kr, ckv_new, kr_new, wkt, wv)


def _merge_kernel(ya_ref, ybp_ref, ybs_ref, ycp_ref, ycs_ref, ga_ref, gb_ref, gc_ref, wa_ref, wb_ref, wc_ref,
                  o_ref, *, n_prompt_tiles):
    i = pl.program_id(1)

    def emit(yb, yc):
        m = jax.nn.sigmoid(ga_ref[...]) * _dot(ya_ref[...], wa_ref[...])
        m = m + jax.nn.sigmoid(gb_ref[...]) * _dot(yb, wb_ref[...])
        m = m + jax.nn.sigmoid(gc_ref[...]) * _dot(yc, wc_ref[...])
        o_ref[...] = m.astype(BF16)

    @pl.when(i < n_prompt_tiles)
    def _():
        emit(ybp_ref[...], ycp_ref[...])

    @pl.when(i >= n_prompt_tiles)
    def _():
        emit(ybs_ref[...], ycs_ref[...])


def _merge(ya, yb_p, yb_s, yc_p, yc_s, proj, wa, wb, wc):
    t = ya.shape[0]
    tp, ts = yb_p.shape[0], yb_s.shape[0]
    tm = _row_tile(math.gcd(tp, ts), 256)
    npt = tp // tm
    tn = 1024
    nn = D_MODEL // tn
    g0 = COL_GATES // tn
    return pl.pallas_call(
        functools.partial(_merge_kernel, n_prompt_tiles=npt),
        grid=(nn, t // tm),
        in_specs=[
            pl.BlockSpec((tm, LRU_WIDTH), lambda j, i: (i, 0)),
            pl.BlockSpec((tm, MLA_HEADS * V_HEAD), lambda j, i: (jnp.minimum(i, npt - 1), 0)),
            pl.BlockSpec((tm, MLA_HEADS * V_HEAD), lambda j, i: (jnp.maximum(i - npt, 0), 0)),
            pl.BlockSpec((tm, S5_WIDTH), lambda j, i: (jnp.minimum(i, npt - 1), 0)),
            pl.BlockSpec((tm, S5_WIDTH), lambda j, i: (jnp.maximum(i - npt, 0), 0)),
            pl.BlockSpec((tm, tn), lambda j, i: (i, g0 + j)),
            pl.BlockSpec((tm, tn), lambda j, i: (i, g0 + nn + j)),
            pl.BlockSpec((tm, tn), lambda j, i: (i, g0 + 2 * nn + j)),
            pl.BlockSpec((LRU_WIDTH, tn), lambda j, i: (0, j)),
            pl.BlockSpec((MLA_HEADS * V_HEAD, tn), lambda j, i: (0, j)),
            pl.BlockSpec((S5_WIDTH, tn), lambda j, i: (0, j)),
        ],
        out_specs=pl.BlockSpec((tm, tn), lambda j, i: (i, j)),
        out_shape=jax.ShapeDtypeStruct((t, D_MODEL), BF16),
        compiler_params=_cparams(("parallel", "parallel")),
        name="branch_merge",
    )(ya, yb_p, yb_s, yc_p, yc_s, proj, proj, proj, wa, wb, wc)


def _outln_kernel(x_ref, m_ref, w_ref, bo_ref, g_ref, b_ref, o_ref, ob_ref):
    y = ALPHA * x_ref[...] + _dot(m_ref[...], w_ref[...]) + bo_ref[...]
    y = _ln_rows(y, g_ref[...], b_ref[...])
    o_ref[...] = y
    ob_ref[...] = y.astype(BF16)


def _out_ln(x, merged, w_out, b_out, g, b):
    t = x.shape[0]
    tm = _row_tile(t, 256)
    c2 = lambda i: (0, 0)
    return pl.pallas_call(
        _outln_kernel,
        grid=(t // tm,),
        in_specs=[
            pl.BlockSpec((tm, D_MODEL), lambda i: (i, 0)),
            pl.BlockSpec((tm, D_MODEL), lambda i: (i, 0)),
            pl.BlockSpec((D_MODEL, D_MODEL), c2),
            pl.BlockSpec((1, D_MODEL), c2),
            pl.BlockSpec((1, D_MODEL), c2),
            pl.BlockSpec((1, D_MODEL), c2),
        ],
        out_specs=[pl.BlockSpec((tm, D_MODEL), lambda i: (i, 0)),
                   pl.BlockSpec((tm, D_MODEL), lambda i: (i, 0))],
        out_shape=[jax.ShapeDtypeStruct((t, D_MODEL), F32), jax.ShapeDtypeStruct((t, D_MODEL), BF16)],
        compiler_params=_cparams(("parallel",)),
        name="out_proj_ln1",
    )(x, merged, w_out, b_out.reshape(1, -1), g.reshape(1, -1), b.reshape(1, -1))


def _router_kernel(x_ref, wt_ref, bias_ref, idx_ref, wgt_ref):
    tm = x_ref.shape[0]
    logits = lax.dot_general(wt_ref[...], x_ref[...], (((1,), (1,)), ((), ())),
                             preferred_element_type=F32)
    neg = -jnp.inf
    sub = lax.broadcasted_iota(I32, (GROUP_SIZE, tm), 0)
    scores, choice = [], []
    for g in range(N_GROUPS):
        sg = jax.nn.sigmoid(logits[g * GROUP_SIZE:(g + 1) * GROUP_SIZE, :])
        scores.append(sg)
        choice.append(sg + bias_ref[g * GROUP_SIZE:(g + 1) * GROUP_SIZE, :])

    def colmax(v):
        return jnp.max(v, axis=0, keepdims=True)

    gscore = []
    for g in range(N_GROUPS):
        m1 = colmax(choice[g])
        j1 = jnp.min(jnp.where(choice[g] == m1, sub, GROUP_SIZE), axis=0, keepdims=True)
        m2 = colmax(jnp.where(sub == j1, neg, choice[g]))
        gscore.append(m1 + m2)
    taken = [jnp.zeros((1, tm), jnp.bool_) for _ in range(N_GROUPS)]
    for _ in range(TOPK_GROUPS):
        best = None
        for g in range(N_GROUPS):
            cand = jnp.where(taken[g], neg, gscore[g])
            best = cand if best is None else jnp.maximum(best, cand)
        found = jnp.zeros((1, tm), jnp.bool_)
        for g in range(N_GROUPS):
            pick = jnp.logical_and(jnp.logical_and(gscore[g] == best, ~taken[g]), ~found)
            found = jnp.logical_or(found, pick)
            taken[g] = jnp.logical_or(taken[g], pick)
    masked = [jnp.where(taken[g], choice[g], neg) for g in range(N_GROUPS)]
    eidx = [sub + g * GROUP_SIZE for g in range(N_GROUPS)]
    out_row = lax.broadcasted_iota(I32, (SUBLANES, tm), 0)
    idx_out = jnp.zeros((SUBLANES, tm), I32)
    wgt_out = jnp.zeros((SUBLANES, tm), F32)
    wsum = jnp.zeros((1, tm), F32)
    for k in range(TOP_K):
        mx = masked[0]
        for g in range(1, N_GROUPS):
            mx = jnp.maximum(mx, masked[g])
        mx = colmax(mx)
        cand = jnp.where(masked[0] == mx, eidx[0], N_EXPERTS)
        for g in range(1, N_GROUPS):
            cand = jnp.minimum(cand, jnp.where(masked[g] == mx, eidx[g], N_EXPERTS))
        ik = jnp.min(cand, axis=0, keepdims=True)
        wk = jnp.where(eidx[0] == ik, scores[0], 0.0)
        for g in range(1, N_GROUPS):
            wk = wk + jnp.where(eidx[g] == ik, scores[g], 0.0)
        wk = jnp.sum(wk, axis=0, keepdims=True)
        masked = [jnp.where(eidx[g] == ik, neg, masked[g]) for g in range(N_GROUPS)]
        wsum = wsum + wk
        idx_out = jnp.where(out_row == k, ik, idx_out)
        wgt_out = jnp.where(out_row == k, wk, wgt_out)
    idx_ref[...] = idx_out
    wgt_ref[...] = wgt_out / wsum * ROUTE_SCALE


def _router(xb, w, bias):
    t = xb.shape[0]
    tm = _row_tile(t, 512)
    bias_b = jnp.broadcast_to(bias.astype(F32)[:, None], (N_EXPERTS, tm))
    return pl.pallas_call(
        _router_kernel,
        grid=(t // tm,),
        in_specs=[pl.BlockSpec((tm, D_MODEL), lambda i: (i, 0)),
                  pl.BlockSpec((N_EXPERTS, D_MODEL), lambda i: (0, 0)),
                  pl.BlockSpec((N_EXPERTS, tm), lambda i: (0, 0))],
        out_specs=[pl.BlockSpec((SUBLANES, tm), lambda i: (0, i)),
                   pl.BlockSpec((SUBLANES, tm), lambda i: (0, i))],
        out_shape=[jax.ShapeDtypeStruct((SUBLANES, t), I32), jax.ShapeDtypeStruct((SUBLANES, t), F32)],
        compiler_params=_cparams(("parallel",)),
        name="moe_router",
    )(xb, w.T, bias_b)


def _expert_kernel(blk_e_ref, nvalid_ref, tok_ref, tokn_ref, slot_ref, x_hbm, w1_ref, w3_ref, w2_ref,
                   o_hbm, xbuf, ybuf, w1b, w3b, w2b, gsem, ssem, *, rows):
    b = pl.program_id(0)
    nb = pl.num_programs(0)
    cur = b % 2
    nxt = 1 - cur

    def start_gather(tref, buf):
        for j in range(rows):
            pltpu.make_async_copy(x_hbm.at[pl.ds(tref[0, 0, j], 1), :], xbuf.at[buf, pl.ds(j, 1), :],
                                  gsem.at[buf]).start(priority=j % 2)

    def wait_gather(buf):
        pltpu.make_async_copy(x_hbm.at[pl.ds(0, rows), :], xbuf.at[buf], gsem.at[buf]).wait()

    def scatter_row(j, slot, buf):
        return pltpu.make_async_copy(ybuf.at[buf, pl.ds(j, 1), :], o_hbm.at[pl.ds(slot, 1), :], ssem.at[buf])

    def start_scatter(count, buf):
        def body(j, carry):
            scatter_row(j, slot_ref[0, 0, j], buf).start()
            return carry

        @pl.when(count == rows)
        def _():
            for j in range(rows):
                scatter_row(j, slot_ref[0, 0, j], buf).start(priority=j % 2)

        @pl.when(count < rows)
        def _():
            lax.fori_loop(0, count, body, 0)

    def wait_scatter(count, buf):
        @pl.when(count == rows)
        def _():
            pltpu.make_async_copy(ybuf.at[buf], o_hbm.at[pl.ds(0, rows), :], ssem.at[buf]).wait()

        @pl.when(count < rows)
        def _():
            def body(j, carry):
                scatter_row(0, 0, buf).wait()
                return carry
            lax.fori_loop(0, count, body, 0)

    @pl.when(b == 0)
    def _():
        start_gather(tok_ref, cur)

    @pl.when(b + 1 < nb)
    def _():
        start_gather(tokn_ref, nxt)

    wait_gather(cur)

    @pl.when(b >= 2)
    def _():
        wait_scatter(nvalid_ref[jnp.maximum(b - 2, 0)], cur)

    @pl.when(jnp.logical_or(b == 0, blk_e_ref[b] != blk_e_ref[jnp.maximum(b - 1, 0)]))
    def _():
        w1b[...] = w1_ref[0, 0].astype(BF16)
        w3b[...] = w3_ref[0, 0].astype(BF16)
        w2b[...] = w2_ref[0, 0].astype(BF16)

    nv = nvalid_ref[b]

    @pl.when(nv > 0)
    def _():
        x = xbuf[cur].astype(BF16)
        h = jax.nn.silu(_dot(x, w1b[...])) * _dot(x, w3b[...])
        ybuf[cur] = _dot(h.astype(BF16), w2b[...])
        start_scatter(nv, cur)

    @pl.when(b == nb - 1)
    def _():
        wait_scatter(nv, cur)

        @pl.when(b >= 1)
        def _():
            wait_scatter(nvalid_ref[jnp.maximum(b - 1, 0)], nxt)


def _experts(x, w1, w3, w2, layer, blk_e, nvalid, buf_tok, buf_slot, *, n_out_rows):
    rows = EXPERT_ROWS
    n_blocks = blk_e.shape[0]
    tok3 = buf_tok.reshape(n_blocks, 1, rows)
    slot3 = buf_slot.reshape(n_blocks, 1, rows)
    smem_blk = lambda f: pl.BlockSpec((1, 1, rows), f, memory_space=pltpu.SMEM)
    grid_spec = pltpu.PrefetchScalarGridSpec(
        num_scalar_prefetch=2,
        grid=(n_blocks,),
        in_specs=[
            smem_blk(lambda b, be, nv: (b, 0, 0)),
            smem_blk(lambda b, be, nv: (jnp.minimum(b + 1, n_blocks - 1), 0, 0)),
            smem_blk(lambda b, be, nv: (b, 0, 0)),
            pl.BlockSpec(memory_space=pl.ANY),
            pl.BlockSpec((1, 1, D_MODEL, EXPERT_FF), lambda b, be, nv: (layer, be[b], 0, 0)),
            pl.BlockSpec((1, 1, D_MODEL, EXPERT_FF), lambda b, be, nv: (layer, be[b], 0, 0)),
            pl.BlockSpec((1, 1, EXPERT_FF, D_MODEL), lambda b, be, nv: (layer, be[b], 0, 0)),
        ],
        out_specs=pl.BlockSpec(memory_space=pl.ANY),
        scratch_shapes=[pltpu.VMEM((2, rows, D_MODEL), F32), pltpu.VMEM((2, rows, D_MODEL), F32),
                        pltpu.VMEM((D_MODEL, EXPERT_FF), BF16), pltpu.VMEM((D_MODEL, EXPERT_FF), BF16),
                        pltpu.VMEM((EXPERT_FF, D_MODEL), BF16),
                        pltpu.SemaphoreType.DMA((2,)), pltpu.SemaphoreType.DMA((2,))],
    )
    return pl.pallas_call(
        functools.partial(_expert_kernel, rows=rows),
        grid_spec=grid_spec,
        out_shape=jax.ShapeDtypeStruct((n_out_rows, D_MODEL), F32),
        compiler_params=_cparams(("arbitrary",)),
        name="moe_experts",
    )(blk_e, nvalid, tok3, tok3, slot3, x, w1, w3, w2)


def _combine_kernel(x_ref, xb_ref, wt_ref, y0, y1, y2, y3, y4, y5, s1_ref, s3_ref, s2_ref, g_ref, b_ref,
                    o_ref, ob_ref):
    wt = wt_ref[...]
    routed = None
    for k, yk in enumerate((y0, y1, y2, y3, y4, y5)):
        term = yk[...] * wt[:, k:k + 1]
        routed = term if routed is None else routed + term
    xb = xb_ref[...]
    hs = jax.nn.silu(_dot(xb, s1_ref[...])) * _dot(xb, s3_ref[...])
    shared = _dot(hs.astype(BF16), s2_ref[...])
    y = _ln_rows(ALPHA * x_ref[...] + (routed + shared), g_ref[...], b_ref[...])
    o_ref[...] = y
    ob_ref[...] = y.astype(BF16)


def _combine(x, xb, wt, y6, s1, s3, s2, g, b):
    t = x.shape[0]
    tm = _row_tile(t, 128)
    nt = t // tm
    c2 = lambda i: (0, 0)
    yspecs = [pl.BlockSpec((tm, D_MODEL), (lambda i, k=k: (k * nt + i, 0))) for k in range(TOP_K)]
    return pl.pallas_call(
        _combine_kernel,
        grid=(nt,),
        in_specs=[
            pl.BlockSpec((tm, D_MODEL), lambda i: (i, 0)),
            pl.BlockSpec((tm, D_MODEL), lambda i: (i, 0)),
            pl.BlockSpec((tm, SUBLANES), lambda i: (i, 0)),
            *yspecs,
            pl.BlockSpec((D_MODEL, EXPERT_FF), c2),
            pl.BlockSpec((D_MODEL, EXPERT_FF), c2),
            pl.BlockSpec((EXPERT_FF, D_MODEL), c2),
            pl.BlockSpec((1, D_MODEL), c2),
            pl.BlockSpec((1, D_MODEL), c2),
        ],
        out_specs=[pl.BlockSpec((tm, D_MODEL), lambda i: (i, 0)),
                   pl.BlockSpec((tm, D_MODEL), lambda i: (i, 0))],
        out_shape=[jax.ShapeDtypeStruct((t, D_MODEL), F32), jax.ShapeDtypeStruct((t, D_MODEL), BF16)],
        compiler_params=_cparams(("parallel",)),
        name="moe_combine_ln2",
    )(x, xb, wt, y6, y6, y6, y6, y6, y6, s1, s3, s2, g.reshape(1, -1), b.reshape(1, -1))


def _route_plan(idx, t):
    rows = EXPERT_ROWS
    tk = t * TOP_K
    n_blocks = -(-(tk + N_EXPERTS * (rows - 1)) // rows)
    n_rows = n_blocks * rows
    flat_e = idx.reshape(-1)
    order = jnp.argsort(flat_e).astype(I32)
    counts = jnp.zeros((N_EXPERTS,), I32).at[flat_e].add(1)
    padded = (counts + rows - 1) // rows * rows
    pad_end = jnp.cumsum(padded)
    pad_start = pad_end - padded
    start = jnp.cumsum(counts) - counts
    blk = jnp.arange(n_blocks, dtype=I32)
    blk_e = jnp.minimum(jnp.sum(blk[:, None] * rows >= pad_end[None, :], axis=1), N_EXPERTS - 1).astype(I32)
    in_blk = jnp.arange(rows, dtype=I32)[None, :]
    j = blk[:, None] * rows + in_blk - pad_start[blk_e][:, None]
    valid = jnp.logical_and(j < counts[blk_e][:, None], (blk * rows < pad_end[N_EXPERTS - 1])[:, None])
    src = jnp.clip(start[blk_e][:, None] + j, 0, tk - 1)
    flat = order[src]
    tok = flat // TOP_K
    kk = flat - tok * TOP_K
    buf_tok = jnp.where(valid, tok, 0).astype(I32).reshape(-1)
    buf_slot = jnp.where(valid, kk * t + tok, 0).astype(I32).reshape(-1)
    nvalid = jnp.sum(valid, axis=1).astype(I32)
    return blk_e, nvalid, buf_tok, buf_slot


def _prep_layer(l, p):
    f32 = F32
    w_in = p["w_in"][l]
    offs = np.cumsum((0, 1024, 1024, 512, 512, 64, 1024))
    xa, ga, cq, ckv, kr, uc, gates = (w_in[:, offs[0]:offs[1]], w_in[:, offs[1]:offs[2]], w_in[:, offs[2]:offs[3]],
                                      w_in[:, offs[3]:offs[4]], w_in[:, offs[4]:offs[5]], w_in[:, offs[5]:offs[6]],
                                      w_in[:, offs[6]:])
    half = QK_ROPE // 2
    kr_sw = jnp.concatenate([kr[:, half:], kr[:, :half]], axis=1)
    w_in_aug = jnp.concatenate([xa, ga, cq, ckv, uc, gates, kr, kr_sw], axis=1).astype(BF16)

    wuq = p["mla_w_uq"][l]
    rope_w = wuq[:, :, QK_NOPE:]
    wuq_aug = jnp.concatenate([wuq, rope_w[:, :, half:], rope_w[:, :, :half]], axis=2)
    wuq_aug = jnp.transpose(wuq_aug, (1, 0, 2)).astype(BF16)
    wuk = jnp.transpose(p["mla_w_uk"][l], (1, 0, 2)).astype(BF16)
    wukt = jnp.transpose(p["mla_w_uk"][l], (1, 2, 0)).astype(BF16)
    wuv = jnp.transpose(p["mla_w_uv"][l], (1, 0, 2)).astype(BF16)

    wg = jnp.concatenate([p["lru_gate_a_w"][l], p["lru_gate_x_w"][l]], axis=-1).astype(BF16)
    bg = jnp.concatenate([p["lru_gate_a_b"][l], p["lru_gate_x_b"][l]], axis=-1)[:, None, :]
    sp = jax.nn.softplus(-p["lru_lambda"][l].astype(f32))

    dt = jnp.exp(p["s5_log_dt"][l].astype(f32))[:, None]
    lam_re = p["s5_lam_re"][l].astype(f32)
    lam_im = p["s5_lam_im"][l].astype(f32)
    mag = jnp.exp(lam_re * dt)
    a_re = mag * jnp.cos(lam_im * dt)
    a_im = mag * jnp.sin(lam_im * dt)
    den = jnp.square(lam_re) + jnp.square(lam_im)
    f_re = ((a_re - 1.0) * lam_re + a_im * lam_im) / den
    f_im = (a_im * lam_re - (a_re - 1.0) * lam_im) / den
    b_re = p["s5_b_re"][l].astype(f32)
    b_im = p["s5_b_im"][l].astype(f32)
    bb_re = f_re[..., None] * b_re - f_im[..., None] * b_im
    bb_im = f_re[..., None] * b_im + f_im[..., None] * b_re
    gpr = S5_GROUPS // S5_RG
    eye = jnp.eye(gpr, dtype=f32)

    def in_bd(bb):
        bbr = bb.reshape(S5_RG, gpr, S5_STATE, S5_GROUP)
        return jnp.einsum("rgpc,gh->rgchp", bbr, eye).reshape(S5_RG, gpr * S5_GROUP, gpr * S5_STATE).astype(BF16)

    def out_bd(cm):
        cr = cm.astype(f32).reshape(S5_RG, gpr, S5_GROUP, S5_STATE)
        return jnp.einsum("rgcp,gh->rgphc", cr, eye).reshape(S5_RG, gpr * S5_STATE, gpr * S5_GROUP).astype(BF16)

    s5 = dict(bbr=in_bd(bb_re), bbi=in_bd(bb_im), ccr=out_bd(p["s5_c_re"][l]), cci=out_bd(p["s5_c_im"][l]),
              ar=a_re.reshape(1, -1), ai=a_im.reshape(1, -1), d=p["s5_d"][l].astype(f32).reshape(1, -1),
              glu_w=p["s5_glu_w"][l].astype(BF16), glu_b=p["s5_glu_b"][l].reshape(1, -1))
    return dict(
        w_in=w_in_aug, wuq=wuq_aug, wuk=wuk, wukt=wukt, wuv=wuv, wg=wg, bg=bg, sp=sp, s5=s5,
        wa=p["w_branch_a"][l].astype(BF16), wb=p["w_branch_b"][l].astype(BF16), wc=p["w_branch_c"][l].astype(BF16),
        w_out=p["w_out"][l].astype(BF16), router_w=p["router_w"][l].astype(BF16),
        s1=p["sh_w1"][l].astype(BF16), s3=p["sh_w3"][l].astype(BF16), s2=p["sh_w2"][l].astype(BF16),
    )


def _rope_tables(bp, sp_len, bs, ss_len, past):
    half = QK_ROPE // 2
    inv = ROPE_THETA ** (-jnp.arange(half, dtype=F32) / half)
    pos = jnp.concatenate([jnp.tile(jnp.arange(sp_len, dtype=I32), bp),
                           jnp.tile(past + jnp.arange(ss_len, dtype=I32), bs)])
    ang = pos.astype(F32)[:, None] * inv
    cos, sin = jnp.cos(ang), jnp.sin(ang)
    return jnp.concatenate([cos, cos], axis=1), jnp.concatenate([-sin, sin], axis=1)


def kernel(x_prompt, x_sample, cache_mla_ckv, cache_mla_krope, state_lru_conv, state_lru_h, state_s5_re, state_s5_im, ln_in_g, ln_in_b, w_in, lru_conv_w, lru_conv_b, lru_gate_a_w, lru_gate_a_b, lru_gate_x_w, lru_gate_x_b, lru_lambda, mla_q_norm_g, mla_w_uq, mla_kv_norm_g, mla_w_uk, mla_w_uv, s5_lam_re, s5_lam_im, s5_log_dt, s5_b_re, s5_b_im, s5_c_re, s5_c_im, s5_d, s5_glu_w, s5_glu_b, w_branch_a, w_branch_b, w_branch_c, w_out, b_out, ln1_g, ln1_b, router_w, router_bias, exp_w1, exp_w3, exp_w2, sh_w1, sh_w3, sh_w2, ln2_g, ln2_b):
    p = dict(w_in=w_in, lru_gate_a_w=lru_gate_a_w, lru_gate_a_b=lru_gate_a_b, lru_gate_x_w=lru_gate_x_w,
             lru_gate_x_b=lru_gate_x_b, lru_lambda=lru_lambda, mla_w_uq=mla_w_uq, mla_w_uk=mla_w_uk,
             mla_w_uv=mla_w_uv, s5_lam_re=s5_lam_re, s5_lam_im=s5_lam_im, s5_log_dt=s5_log_dt, s5_b_re=s5_b_re,
             s5_b_im=s5_b_im, s5_c_re=s5_c_re, s5_c_im=s5_c_im, s5_d=s5_d, s5_glu_w=s5_glu_w, s5_glu_b=s5_glu_b,
             w_branch_a=w_branch_a, w_branch_b=w_branch_b, w_branch_c=w_branch_c, w_out=w_out, router_w=router_w,
             sh_w1=sh_w1, sh_w3=sh_w3, sh_w2=sh_w2)
    bp, sp_len, d = x_prompt.shape
    bs, ss_len, _ = x_sample.shape
    past = cache_mla_ckv.shape[2]
    depth = w_in.shape[0]
    assert d == D_MODEL and sp_len % CHUNK == 0 and ss_len == CHUNK
    tp, ts = bp * sp_len, bs * ss_len
    t = tp + ts

    cos2, sin2 = _rope_tables(bp, sp_len, bs, ss_len, past)
    x, xb = _ln_in(x_prompt.reshape(tp, d), x_sample.reshape(ts, d), ln_in_g, ln_in_b)

    new_p, new_s = [], []
    for l in range(depth):
        lp = _prep_layer(l, p)
        proj = _in_proj(xb, lp["w_in"])

        conv0 = jnp.concatenate([jnp.zeros((bp, CONV_WIDTH - 1, LRU_WIDTH), F32), state_lru_conv[l]], axis=0)
        h0 = jnp.concatenate([jnp.zeros((bp, LRU_WIDTH), F32), state_lru_h[l]], axis=0)[:, None, :]
        y_a, conv_new, h_new = _lru_branch(proj, conv0, h0, lru_conv_w[l], lru_conv_b[l], lp["wg"], lp["bg"],
                                           lp["sp"], bp=bp, sp_len=sp_len, bs=bs)
        h_new = h_new[:, 0, :]

        q = _q_proj(proj, mla_q_norm_g[l], lp["wuq"], cos2, sin2)
        ckv_new, kr_new, k, v = _kv_proj(proj, mla_kv_norm_g[l], lp["wuk"], lp["wuv"], cos2, sin2)
        yb_p = _attn_prompt(q, k, v, bp=bp, sp_len=sp_len)
        yb_s = _attn_sample(q, ckv_new, kr_new, cache_mla_ckv, cache_mla_krope, l, lp["wukt"], lp["wuv"],
                            bs=bs, ss_len=ss_len, past=past, tp=tp)

        zero_state = jnp.zeros((bp, S5_HID), F32)
        yc_p, s5r_p, s5i_p = _s5_branch(proj, 0, zero_state, zero_state, lp["s5"], nb=bp, seq_len=sp_len)
        yc_s, s5r_s, s5i_s = _s5_branch(proj, tp, state_s5_re[l].reshape(bs, S5_HID),
                                        state_s5_im[l].reshape(bs, S5_HID), lp["s5"], nb=bs, seq_len=ss_len)
        yc_p = yc_p.reshape(tp, S5_WIDTH)
        yc_s = yc_s.reshape(ts, S5_WIDTH)

        merged = _merge(y_a, yb_p, yb_s, yc_p, yc_s, proj, lp["wa"], lp["wb"], lp["wc"])
        x, xb = _out_ln(x, merged, lp["w_out"], b_out[l], ln1_g[l], ln1_b[l])

        idx8, wt8 = _router(xb, lp["router_w"], router_bias[l])
        blk_e, nvalid, buf_tok, buf_slot = _route_plan(idx8[:TOP_K].T, t)
        y6 = _experts(x, exp_w1, exp_w3, exp_w2, l, blk_e, nvalid, buf_tok, buf_slot, n_out_rows=TOP_K * t)
        x, xb = _combine(x, xb, wt8.T, y6, lp["s1"], lp["s3"], lp["s2"], ln2_g[l], ln2_b[l])

        new_p.append((ckv_new[:tp].reshape(bp, sp_len, KV_LORA), kr_new[:tp].reshape(bp, sp_len, QK_ROPE),
                      conv_new[:bp], h_new[:bp], s5r_p.reshape(bp, S5_GROUPS, S5_STATE),
                      s5i_p.reshape(bp, S5_GROUPS, S5_STATE)))
        new_s.append((ckv_new[tp:].reshape(bs, ss_len, KV_LORA), kr_new[tp:].reshape(bs, ss_len, QK_ROPE),
                      conv_new[bp:], h_new[bp:], s5r_s.reshape(bs, S5_GROUPS, S5_STATE),
                      s5i_s.reshape(bs, S5_GROUPS, S5_STATE)))

    p_out = [jnp.stack(z) for z in zip(*new_p)]
    s_out = [jnp.stack(z) for z in zip(*new_s)]
    return (x[:tp].reshape(bp, sp_len, d), x[tp:].reshape(bs, ss_len, d), *p_out, *s_out)
```

```python
import functools
import math

import jax
import jax.numpy as jnp
import numpy as np
from jax import lax
from jax.experimental import pallas as pl
from jax.experimental.pallas import tpu as pltpu

F32 = jnp.float32
BF16 = jnp.bfloat16
I32 = jnp.int32

D_MODEL = 2048
DEPTH = 2
CHUNK = 64
LN_EPS = 1e-5
RMS_EPS = 1e-6
ALPHA = (2 * DEPTH) ** 0.25
LRU_WIDTH = 1024
LRU_BLOCKS = 8
LRU_BLOCK = 128
CONV_WIDTH = 4
LRU_C = 8.0
MLA_HEADS = 16
QK_NOPE = 128
QK_ROPE = 64
QK_DIM = QK_NOPE + QK_ROPE
V_HEAD = 128
Q_LORA = 512
KV_LORA = 512
ROPE_THETA = 10000.0
S5_WIDTH = 1024
S5_GROUP = 16
S5_GROUPS = 64
S5_STATE = 64
S5_HID = S5_GROUPS * S5_STATE
N_EXPERTS = 64
TOP_K = 6
N_GROUPS = 8
GROUP_SIZE = N_EXPERTS // N_GROUPS
TOPK_GROUPS = 4
EXPERT_FF = 512
ROUTE_SCALE = 2.5

COL_XA, COL_GA, COL_CQ, COL_CKV, COL_UC, COL_GATES, COL_KR = 0, 1024, 2048, 2560, 3072, 4096, 10240
N_PROJ = 10368

V7X_VMEM_BYTES = 64 * 1024 * 1024
VMEM_LIMIT = V7X_VMEM_BYTES - 8 * 1024 * 1024
LANES = 128
SUBLANES = 8

EXPERT_ROWS = 256
ATTN_TQ = 1024
ATTN_HEADS_PER_STEP = 2


def _cparams(sem):
    return pltpu.CompilerParams(dimension_semantics=sem, vmem_limit_bytes=VMEM_LIMIT)


def _row_tile(n, pref):
    for t in (1024, 512, 256, 128, 64, 32, 16, 8):
        if t <= pref and n % t == 0:
            return t
    raise ValueError(f"no row tile for {n}")


def _ln_rows(x, g, b):
    mu = jnp.mean(x, axis=-1, keepdims=True)
    xc = x - mu
    var = jnp.mean(xc * xc, axis=-1, keepdims=True)
    return xc * lax.rsqrt(var + LN_EPS) * g + b


def _dot(a, b):
    return jnp.dot(a, b, preferred_element_type=F32)


def _ln_in_kernel(xp_ref, xs_ref, g_ref, b_ref, o_ref, ob_ref, *, n_prompt_tiles):
    i = pl.program_id(0)

    def emit(x):
        y = _ln_rows(x, g_ref[...], b_ref[...])
        o_ref[...] = y
        ob_ref[...] = y.astype(BF16)

    @pl.when(i < n_prompt_tiles)
    def _():
        emit(xp_ref[...])

    @pl.when(i >= n_prompt_tiles)
    def _():
        emit(xs_ref[...])


def _ln_in(xp, xs, g, b):
    tp, ts = xp.shape[0], xs.shape[0]
    tm = _row_tile(math.gcd(tp, ts), 512)
    npt, nst = tp // tm, ts // tm
    t = tp + ts
    return pl.pallas_call(
        functools.partial(_ln_in_kernel, n_prompt_tiles=npt),
        grid=(npt + nst,),
        in_specs=[
            pl.BlockSpec((tm, D_MODEL), lambda i: (jnp.minimum(i, npt - 1), 0)),
            pl.BlockSpec((tm, D_MODEL), lambda i: (jnp.maximum(i - npt, 0), 0)),
            pl.BlockSpec((1, D_MODEL), lambda i: (0, 0)),
            pl.BlockSpec((1, D_MODEL), lambda i: (0, 0)),
        ],
        out_specs=[pl.BlockSpec((tm, D_MODEL), lambda i: (i, 0)),
                   pl.BlockSpec((tm, D_MODEL), lambda i: (i, 0))],
        out_shape=[jax.ShapeDtypeStruct((t, D_MODEL), F32), jax.ShapeDtypeStruct((t, D_MODEL), BF16)],
        compiler_params=_cparams(("parallel",)),
        name="ln_in",
    )(xp, xs, g.reshape(1, -1), b.reshape(1, -1))


def _mm_kernel(x_ref, w_ref, o_ref):
    o_ref[...] = _dot(x_ref[...], w_ref[...])


def _in_proj(xb, w):
    t, k = xb.shape
    n = w.shape[1]
    tm = _row_tile(t, 1024)
    tn = 1152
    assert n % tn == 0
    return pl.pallas_call(
        _mm_kernel,
        grid=(n // tn, t // tm),
        in_specs=[pl.BlockSpec((tm, k), lambda j, i: (i, 0)),
                  pl.BlockSpec((k, tn), lambda j, i: (0, j))],
        out_specs=pl.BlockSpec((tm, tn), lambda j, i: (i, j)),
        out_shape=jax.ShapeDtypeStruct((t, n), F32),
        compiler_params=_cparams(("parallel", "parallel")),
        name="in_proj",
    )(xb, w)


def _lru_kernel(xa_ref, ga_ref, cs_ref, h0_ref, cw_ref, cb_ref, wg_ref, bg_ref, sp_ref,
                y_ref, cn_ref, hn_ref, xbuf, hcar, *, rows, n_prompt_tiles, tiles_per_seq):
    c = pl.program_id(0)
    first = jnp.logical_or(c >= n_prompt_tiles, c % tiles_per_seq == 0)

    @pl.when(first)
    def _():
        xbuf[5:8, :] = cs_ref[0]
        hcar[0:1, :] = h0_ref[0]

    xa = xa_ref[...]
    xbuf[8:8 + rows, :] = xa
    cw = cw_ref[...]
    xc = (cb_ref[...] + cw[0:1] * xbuf[5:5 + rows, :] + cw[1:2] * xbuf[6:6 + rows, :]
          + cw[2:3] * xbuf[7:7 + rows, :] + cw[3:4] * xa)
    tail = xa[rows - 3:rows, :]
    xbuf[5:8, :] = tail
    cn_ref[0] = tail

    row = lax.broadcasted_iota(I32, (rows, LRU_BLOCK), 0)
    for n in range(LRU_BLOCKS):
        sl = slice(n * LRU_BLOCK, (n + 1) * LRU_BLOCK)
        xcb = xc[:, sl]
        g = _dot(xcb.astype(BF16), wg_ref[n]) + bg_ref[n]
        r = jax.nn.sigmoid(g[:, :LRU_BLOCK])
        gi = jax.nn.sigmoid(g[:, LRU_BLOCK:])
        log_a = -LRU_C * r * sp_ref[:, sl]
        a = jnp.exp(log_a)
        th = jnp.tanh(log_a)
        b = jnp.sqrt(-2.0 * th / (1.0 - th)) * (gi * xcb)
        s = 1
        while s < rows:
            keep = row >= s
            a_s = jnp.where(keep, pltpu.roll(a, s, 0), 1.0)
            b_s = jnp.where(keep, pltpu.roll(b, s, 0), 0.0)
            b = a * b_s + b
            a = a * a_s
            s *= 2
        h = a * hcar[0:1, sl] + b
        hcar[0:1, sl] = h[rows - 1:rows, :]
        y_ref[:, sl] = (h * jax.nn.gelu(ga_ref[:, sl])).astype(BF16)
    hn_ref[0] = hcar[0:1, :]


def _lru_branch(proj, conv_state, h_state, conv_w, conv_b, wg, bg, sp, *, bp, sp_len, bs):
    t = proj.shape[0]
    rows = CHUNK
    tps = sp_len // rows
    npt = bp * tps
    nt = t // rows
    nseq = bp + bs
    w = LRU_WIDTH

    def seq_of(c):
        return jnp.where(c < npt, c // tps, bp + (c - npt))

    return pl.pallas_call(
        functools.partial(_lru_kernel, rows=rows, n_prompt_tiles=npt, tiles_per_seq=tps),
        grid=(nt,),
        in_specs=[
            pl.BlockSpec((rows, w), lambda c: (c, COL_XA // w)),
            pl.BlockSpec((rows, w), lambda c: (c, COL_GA // w)),
            pl.BlockSpec((1, 3, w), lambda c: (seq_of(c), 0, 0)),
            pl.BlockSpec((1, 1, w), lambda c: (seq_of(c), 0, 0)),
            pl.BlockSpec((CONV_WIDTH, w), lambda c: (0, 0)),
            pl.BlockSpec((1, w), lambda c: (0, 0)),
            pl.BlockSpec((LRU_BLOCKS, LRU_BLOCK, 2 * LRU_BLOCK), lambda c: (0, 0, 0)),
            pl.BlockSpec((LRU_BLOCKS, 1, 2 * LRU_BLOCK), lambda c: (0, 0, 0)),
            pl.BlockSpec((1, w), lambda c: (0, 0)),
        ],
        out_specs=[
            pl.BlockSpec((rows, w), lambda c: (c, 0)),
            pl.BlockSpec((1, 3, w), lambda c: (seq_of(c), 0, 0)),
            pl.BlockSpec((1, 1, w), lambda c: (seq_of(c), 0, 0)),
        ],
        out_shape=[jax.ShapeDtypeStruct((t, w), BF16),
                   jax.ShapeDtypeStruct((nseq, 3, w), F32),
                   jax.ShapeDtypeStruct((nseq, 1, w), F32)],
        scratch_shapes=[pltpu.VMEM((rows + 8, w), F32), pltpu.VMEM((SUBLANES, w), F32)],
        compiler_params=_cparams(("arbitrary",)),
        name="rglru",
    )(proj, proj, conv_state, h_state, conv_w, conv_b.reshape(1, -1), wg, bg, sp.reshape(1, -1))


S5_COLS = 512
S5_RG = 4


def _s5_kernel(*refs, nb, steps):
    u_refs = refs[:nb]
    (h0r_ref, h0i_ref, bbr_ref, bbi_ref, ccr_ref, cci_ref, ar_ref, ai_ref, d_ref, gw_ref, gb_ref,
     o_ref, hr_out, hi_out, bur, bui, hr, hi, tm_in, tm_out) = refs[nb:]
    i = pl.program_id(0)
    n_slab = S5_WIDTH // LANES

    @pl.when(i == 0)
    def _():
        hr[...] = h0r_ref[...]
        hi[...] = h0i_ref[...]

    for b in range(nb):
        ub = u_refs[b][...]
        for s in range(n_slab):
            tm_in[s, pl.ds(b, steps, stride=nb), :] = ub[:, s * LANES:(s + 1) * LANES]
    u = jnp.concatenate([tm_in[s] for s in range(n_slab)], axis=-1)
    ub = u.astype(BF16)
    kin = S5_WIDTH // S5_RG
    kst = S5_HID // S5_RG
    for r in range(S5_RG):
        ur = ub[:, r * kin:(r + 1) * kin]
        bur[:, r * kst:(r + 1) * kst] = _dot(ur, bbr_ref[r])
        bui[:, r * kst:(r + 1) * kst] = _dot(ur, bbi_ref[r])

    for cc in range(S5_HID // S5_COLS):
        cs = slice(cc * S5_COLS, (cc + 1) * S5_COLS)
        ar = jnp.broadcast_to(ar_ref[:, cs], (SUBLANES, S5_COLS))
        ai = jnp.broadcast_to(ai_ref[:, cs], (SUBLANES, S5_COLS))
        for bg in range(nb // SUBLANES):
            bsl = slice(bg * SUBLANES, (bg + 1) * SUBLANES)

            def body(t, carry, cs=cs, bg=bg, ar=ar, ai=ai):
                h_r, h_i = carry
                rsl = pl.ds(pl.multiple_of(t * nb + bg * SUBLANES, SUBLANES), SUBLANES)
                n_r = ar * h_r - ai * h_i + bur[rsl, cs]
                n_i = ar * h_i + ai * h_r + bui[rsl, cs]
                bur[rsl, cs] = n_r
                bui[rsl, cs] = n_i
                return n_r, n_i

            h_r, h_i = lax.fori_loop(0, steps, body, (hr[bsl, cs], hi[bsl, cs]))
            hr[bsl, cs] = h_r
            hi[bsl, cs] = h_i

    ys = []
    for r in range(S5_RG):
        hrb = bur[:, r * kst:(r + 1) * kst].astype(BF16)
        hib = bui[:, r * kst:(r + 1) * kst].astype(BF16)
        ys.append(_dot(hrb, ccr_ref[r]) - _dot(hib, cci_ref[r]))
    y = jnp.concatenate(ys, axis=-1) + d_ref[...] * u
    z = jax.nn.gelu(y)
    gate = jax.nn.sigmoid(_dot(z.astype(BF16), gw_ref[...]) + gb_ref[...])
    out = z * gate
    for s in range(n_slab):
        tm_out[s] = out[:, s * LANES:(s + 1) * LANES]
    for b in range(nb):
        ob = jnp.concatenate([tm_out[s, pl.ds(b, steps, stride=nb), :] for s in range(n_slab)], axis=-1)
        o_ref[b] = ob.astype(BF16)
    hr_out[...] = hr[...]
    hi_out[...] = hi[...]


def _s5_branch(proj, row_off, h0r, h0i, pw, *, nb, seq_len):
    assert nb % SUBLANES == 0
    steps = max(16, 256 // nb)
    assert seq_len % steps == 0 and row_off % steps == 0
    rows = steps * nb
    kin = S5_WIDTH // S5_RG
    kst = S5_HID // S5_RG
    n_slab = S5_WIDTH // LANES
    const2 = lambda i: (0, 0)
    const3 = lambda i: (0, 0, 0)
    tiles_per_seq = seq_len // steps
    u_specs = [pl.BlockSpec((steps, S5_WIDTH),
                            (lambda i, b=b: (row_off // steps + b * tiles_per_seq + i, COL_UC // S5_WIDTH)))
               for b in range(nb)]
    return pl.pallas_call(
        functools.partial(_s5_kernel, nb=nb, steps=steps),
        grid=(tiles_per_seq,),
        in_specs=[
            *u_specs,
            pl.BlockSpec((nb, S5_HID), const2),
            pl.BlockSpec((nb, S5_HID), const2),
            pl.BlockSpec((S5_RG, kin, kst), const3),
            pl.BlockSpec((S5_RG, kin, kst), const3),
            pl.BlockSpec((S5_RG, kst, kin), const3),
            pl.BlockSpec((S5_RG, kst, kin), const3),
            pl.BlockSpec((1, S5_HID), const2),
            pl.BlockSpec((1, S5_HID), const2),
            pl.BlockSpec((1, S5_WIDTH), const2),
            pl.BlockSpec((S5_WIDTH, S5_WIDTH), const2),
            pl.BlockSpec((1, S5_WIDTH), const2),
        ],
        out_specs=[pl.BlockSpec((nb, steps, S5_WIDTH), lambda i: (0, i, 0)),
                   pl.BlockSpec((nb, S5_HID), const2),
                   pl.BlockSpec((nb, S5_HID), const2)],
        out_shape=[jax.ShapeDtypeStruct((nb, seq_len, S5_WIDTH), BF16),
                   jax.ShapeDtypeStruct((nb, S5_HID), F32),
                   jax.ShapeDtypeStruct((nb, S5_HID), F32)],
        scratch_shapes=[pltpu.VMEM((rows, S5_HID), F32), pltpu.VMEM((rows, S5_HID), F32),
                        pltpu.VMEM((nb, S5_HID), F32), pltpu.VMEM((nb, S5_HID), F32),
                        pltpu.VMEM((n_slab, rows, LANES), F32), pltpu.VMEM((n_slab, rows, LANES), F32)],
        compiler_params=_cparams(("arbitrary",)),
        name="s5",
    )(*([proj] * nb), h0r, h0i, pw["bbr"], pw["bbi"], pw["ccr"], pw["cci"], pw["ar"], pw["ai"], pw["d"],
      pw["glu_w"], pw["glu_b"])


def _rms_rows(x, g):
    return x * lax.rsqrt(jnp.mean(x * x, axis=-1, keepdims=True) + RMS_EPS) * g


def _qproj_kernel(cq_ref, g_ref, w_ref, cos_ref, sin_ref, q_ref):
    xn = _rms_rows(cq_ref[...], g_ref[...]).astype(BF16)
    cos2 = cos_ref[...]
    sin2 = sin_ref[...]
    for h in range(MLA_HEADS):
        r = _dot(xn, w_ref[h])
        q_ref[h, :, 0:QK_NOPE] = r[:, 0:QK_NOPE].astype(BF16)
        rot = r[:, QK_NOPE:QK_DIM] * cos2 + r[:, QK_DIM:QK_DIM + QK_ROPE] * sin2
        q_ref[h, :, QK_NOPE:QK_DIM] = rot.astype(BF16)


def _q_proj(proj, g, w_aug, cos2, sin2):
    t = proj.shape[0]
    tm = _row_tile(t, 256)
    return pl.pallas_call(
        _qproj_kernel,
        grid=(t // tm,),
        in_specs=[
            pl.BlockSpec((tm, Q_LORA), lambda i: (i, COL_CQ // Q_LORA)),
            pl.BlockSpec((1, Q_LORA), lambda i: (0, 0)),
            pl.BlockSpec((MLA_HEADS, Q_LORA, 2 * LANES), lambda i: (0, 0, 0)),
            pl.BlockSpec((tm, QK_ROPE), lambda i: (i, 0)),
            pl.BlockSpec((tm, QK_ROPE), lambda i: (i, 0)),
        ],
        out_specs=pl.BlockSpec((MLA_HEADS, tm, QK_DIM), lambda i: (0, i, 0)),
        out_shape=jax.ShapeDtypeStruct((MLA_HEADS, t, QK_DIM), BF16),
        compiler_params=_cparams(("parallel",)),
        name="mla_q_proj",
    )(proj, g.reshape(1, -1), w_aug, cos2, sin2)


def _kvproj_kernel(ckv_ref, kr_ref, g_ref, wk_ref, wv_ref, cos_ref, sin_ref,
                   ckvn_ref, krn_ref, k_ref, v_ref):
    xn = _rms_rows(ckv_ref[...], g_ref[...])
    ckvn_ref[...] = xn
    xb = xn.astype(BF16)
    kr = kr_ref[...]
    rot = kr[:, 0:QK_ROPE] * cos_ref[...] + kr[:, QK_ROPE:2 * QK_ROPE] * sin_ref[...]
    krn_ref[...] = rot
    rb = rot.astype(BF16)
    for h in range(MLA_HEADS):
        k_ref[h, :, 0:QK_NOPE] = _dot(xb, wk_ref[h]).astype(BF16)
        k_ref[h, :, QK_NOPE:QK_DIM] = rb
        v_ref[h] = _dot(xb, wv_ref[h]).astype(BF16)


def _kv_proj(proj, g, wk, wv, cos2, sin2):
    t = proj.shape[0]
    tm = _row_tile(t, 256)
    c3 = lambda i: (0, 0, 0)
    return pl.pallas_call(
        _kvproj_kernel,
        grid=(t // tm,),
        in_specs=[
            pl.BlockSpec((tm, KV_LORA), lambda i: (i, COL_CKV // KV_LORA)),
            pl.BlockSpec((tm, LANES), lambda i: (i, COL_KR // LANES)),
            pl.BlockSpec((1, KV_LORA), lambda i: (0, 0)),
            pl.BlockSpec((MLA_HEADS, KV_LORA, QK_NOPE), c3),
            pl.BlockSpec((MLA_HEADS, KV_LORA, V_HEAD), c3),
            pl.BlockSpec((tm, QK_ROPE), lambda i: (i, 0)),
            pl.BlockSpec((tm, QK_ROPE), lambda i: (i, 0)),
        ],
        out_specs=[
            pl.BlockSpec((tm, KV_LORA), lambda i: (i, 0)),
            pl.BlockSpec((tm, QK_ROPE), lambda i: (i, 0)),
            pl.BlockSpec((MLA_HEADS, tm, QK_DIM), lambda i: (0, i, 0)),
            pl.BlockSpec((MLA_HEADS, tm, V_HEAD), lambda i: (0, i, 0)),
        ],
        out_shape=[jax.ShapeDtypeStruct((t, KV_LORA), F32),
                   jax.ShapeDtypeStruct((t, QK_ROPE), F32),
                   jax.ShapeDtypeStruct((MLA_HEADS, t, QK_DIM), BF16),
                   jax.ShapeDtypeStruct((MLA_HEADS, t, V_HEAD), BF16)],
        compiler_params=_cparams(("parallel",)),
        name="mla_kv_proj",
    )(proj, proj, g.reshape(1, -1), wk, wv, cos2, sin2)


def _qk(q, k):
    return lax.dot_general(q, k, (((1,), (1,)), ((), ())), preferred_element_type=F32)


def _attn_prompt_kernel(q_ref, k_ref, v_ref, o_ref, *, tq, scale):
    qi = pl.program_id(2)
    c = scale * math.log2(math.e)
    nh = q_ref.shape[0]
    qs = [q_ref[h] for h in range(nh)]

    def kv_rows(ki):
        return pl.ds(pl.multiple_of(ki * tq, tq), tq)

    def scores(h, ki):
        return _qk(qs[h], k_ref[h, kv_rows(ki), :])

    def update(state, s, v):
        m, l, acc = state
        m_new = jnp.maximum(m, jnp.max(s, axis=-1, keepdims=True))
        alpha = jnp.exp2((m - m_new) * c)
        p = jnp.exp2((s - m_new) * c)
        l = alpha * l + jnp.sum(p, axis=-1, keepdims=True)
        acc = alpha * acc + _dot(p.astype(BF16), v)
        return m_new, l, acc

    def body(ki, states):
        return tuple(update(states[h], scores(h, ki), v_ref[h, kv_rows(ki), :]) for h in range(nh))

    init = tuple((jnp.full((tq, 1), -jnp.inf, F32), jnp.zeros((tq, 1), F32), jnp.zeros((tq, V_HEAD), F32))
                 for _ in range(nh))
    states = lax.fori_loop(0, qi, body, init)
    rr = lax.broadcasted_iota(I32, (tq, tq), 0) // CHUNK
    cc = lax.broadcasted_iota(I32, (tq, tq), 1) // CHUNK
    for h in range(nh):
        s = jnp.where(cc <= rr, scores(h, qi), -jnp.inf)
        _, l, acc = update(states[h], s, v_ref[h, kv_rows(qi), :])
        o_ref[:, h * V_HEAD:(h + 1) * V_HEAD] = (acc / l).astype(BF16)


def _attn_prompt(q, k, v, *, bp, sp_len):
    tq = _row_tile(sp_len, ATTN_TQ)
    nq = sp_len // tq
    nh = ATTN_HEADS_PER_STEP
    scale = QK_DIM ** -0.5
    return pl.pallas_call(
        functools.partial(_attn_prompt_kernel, tq=tq, scale=scale),
        grid=(bp, MLA_HEADS // nh, nq),
        in_specs=[
            pl.BlockSpec((nh, tq, QK_DIM), lambda b, h, i: (h, b * nq + i, 0)),
            pl.BlockSpec((nh, sp_len, QK_DIM), lambda b, h, i: (h, b, 0)),
            pl.BlockSpec((nh, sp_len, V_HEAD), lambda b, h, i: (h, b, 0)),
        ],
        out_specs=pl.BlockSpec((tq, nh * V_HEAD), lambda b, h, i: (b * nq + i, h)),
        out_shape=jax.ShapeDtypeStruct((bp * sp_len, MLA_HEADS * V_HEAD), BF16),
        compiler_params=_cparams(("parallel", "parallel", "arbitrary")),
        name="mla_attn_prompt",
    )(q, k, v)


def _attn_sample_kernel(q_ref, cp_ref, rp_ref, cn_ref, rn_ref, wkt_ref, wv_ref, o_ref, *, scale):
    hh, sq = q_ref.shape[0], q_ref.shape[1]
    cp = cp_ref[0, 0].astype(BF16)
    cn = cn_ref[...].astype(BF16)
    rp = rp_ref[0, 0].astype(BF16)
    rn = rn_ref[...].astype(BF16)
    qa = jnp.concatenate([_dot(q_ref[h, :, 0:QK_NOPE], wkt_ref[h]).astype(BF16) for h in range(hh)], axis=0)
    qr = jnp.concatenate([q_ref[h, :, QK_NOPE:QK_DIM] for h in range(hh)], axis=0)
    s1 = (_qk(qa, cp) + _qk(qr, rp)) * scale
    s2 = (_qk(qa, cn) + _qk(qr, rn)) * scale
    m = jnp.maximum(jnp.max(s1, axis=-1, keepdims=True), jnp.max(s2, axis=-1, keepdims=True))
    p1 = jnp.exp(s1 - m)
    p2 = jnp.exp(s2 - m)
    l = jnp.sum(p1, axis=-1, keepdims=True) + jnp.sum(p2, axis=-1, keepdims=True)
    ol = (_dot(p1.astype(BF16), cp) + _dot(p2.astype(BF16), cn)) / l
    for h in range(hh):
        oh = _dot(ol[h * sq:(h + 1) * sq, :].astype(BF16), wv_ref[h])
        o_ref[:, h * V_HEAD:(h + 1) * V_HEAD] = oh.astype(BF16)


def _attn_sample(q, ckv_new, kr_new, cache_ckv, cache_kr, layer, wkt, wv, *, bs, ss_len, past, tp):
    assert ss_len == CHUNK and past % CHUNK == 0 and tp % ss_len == 0
    off = tp // ss_len
    scale = QK_DIM ** -0.5
    hh = MLA_HEADS
    c3 = lambda b: (0, 0, 0)
    return pl.pallas_call(
        functools.partial(_attn_sample_kernel, scale=scale),
        grid=(bs,),
        in_specs=[
            pl.BlockSpec((hh, ss_len, QK_DIM), lambda b: (0, off + b, 0)),
            pl.BlockSpec((1, 1, past, KV_LORA), lambda b: (layer, b, 0, 0)),
            pl.BlockSpec((1, 1, past, QK_ROPE), lambda b: (layer, b, 0, 0)),
            pl.BlockSpec((ss_len, KV_LORA), lambda b: (off + b, 0)),
            pl.BlockSpec((ss_len, QK_ROPE), lambda b: (off + b, 0)),
            pl.BlockSpec((hh, QK_NOPE, KV_LORA), c3),
            pl.BlockSpec((hh, KV_LORA, V_HEAD), c3),
        ],
        out_specs=pl.BlockSpec((ss_len, hh * V_HEAD), lambda b: (b, 0)),
        out_shape=jax.ShapeDtypeStruct((bs * ss_len, hh * V_HEAD), BF16),
        compiler_params=_cparams(("parallel",)),
        name="mla_attn_sample",
    )(q, cache_ckv, cache_kr, ckv_new, kr_new, wkt, wv)


def _merge_kernel(ya_ref, ybp_ref, ybs_ref, ycp_ref, ycs_ref, ga_ref, gb_ref, gc_ref, wa_ref, wb_ref, wc_ref,
                  o_ref, *, n_prompt_tiles):
    i = pl.program_id(1)

    def emit(yb, yc):
        m = jax.nn.sigmoid(ga_ref[...]) * _dot(ya_ref[...], wa_ref[...])
        m = m + jax.nn.sigmoid(gb_ref[...]) * _dot(yb, wb_ref[...])
        m = m + jax.nn.sigmoid(gc_ref[...]) * _dot(yc, wc_ref[...])
        o_ref[...] = m.astype(BF16)

    @pl.when(i < n_prompt_tiles)
    def _():
        emit(ybp_ref[...], ycp_ref[...])

    @pl.when(i >= n_prompt_tiles)
    def _():
        emit(ybs_ref[...], ycs_ref[...])


def _merge(ya, yb_p, yb_s, yc_p, yc_s, proj, wa, wb, wc):
    t = ya.shape[0]
    tp, ts = yb_p.shape[0], yb_s.shape[0]
    tm = _row_tile(math.gcd(tp, ts), 512)
    npt = tp // tm
    tn = 1024
    nn = D_MODEL // tn
    g0 = COL_GATES // tn
    return pl.pallas_call(
        functools.partial(_merge_kernel, n_prompt_tiles=npt),
        grid=(nn, t // tm),
        in_specs=[
            pl.BlockSpec((tm, LRU_WIDTH), lambda j, i: (i, 0)),
            pl.BlockSpec((tm, MLA_HEADS * V_HEAD), lambda j, i: (jnp.minimum(i, npt - 1), 0)),
            pl.BlockSpec((tm, MLA_HEADS * V_HEAD), lambda j, i: (jnp.maximum(i - npt, 0), 0)),
            pl.BlockSpec((tm, S5_WIDTH), lambda j, i: (jnp.minimum(i, npt - 1), 0)),
            pl.BlockSpec((tm, S5_WIDTH), lambda j, i: (jnp.maximum(i - npt, 0), 0)),
            pl.BlockSpec((tm, tn), lambda j, i: (i, g0 + j)),
            pl.BlockSpec((tm, tn), lambda j, i: (i, g0 + nn + j)),
            pl.BlockSpec((tm, tn), lambda j, i: (i, g0 + 2 * nn + j)),
            pl.BlockSpec((LRU_WIDTH, tn), lambda j, i: (0, j)),
            pl.BlockSpec((MLA_HEADS * V_HEAD, tn), lambda j, i: (0, j)),
            pl.BlockSpec((S5_WIDTH, tn), lambda j, i: (0, j)),
        ],
        out_specs=pl.BlockSpec((tm, tn), lambda j, i: (i, j)),
        out_shape=jax.ShapeDtypeStruct((t, D_MODEL), BF16),
        compiler_params=_cparams(("parallel", "parallel")),
        name="branch_merge",
    )(ya, yb_p, yb_s, yc_p, yc_s, proj, proj, proj, wa, wb, wc)


def _outln_kernel(x_ref, m_ref, w_ref, bo_ref, g_ref, b_ref, o_ref, ob_ref):
    y = ALPHA * x_ref[...] + _dot(m_ref[...], w_ref[...]) + bo_ref[...]
    y = _ln_rows(y, g_ref[...], b_ref[...])
    o_ref[...] = y
    ob_ref[...] = y.astype(BF16)


def _out_ln(x, merged, w_out, b_out, g, b):
    t = x.shape[0]
    tm = _row_tile(t, 256)
    c2 = lambda i: (0, 0)
    return pl.pallas_call(
        _outln_kernel,
        grid=(t // tm,),
        in_specs=[
            pl.BlockSpec((tm, D_MODEL), lambda i: (i, 0)),
            pl.BlockSpec((tm, D_MODEL), lambda i: (i, 0)),
            pl.BlockSpec((D_MODEL, D_MODEL), c2),
            pl.BlockSpec((1, D_MODEL), c2),
            pl.BlockSpec((1, D_MODEL), c2),
            pl.BlockSpec((1, D_MODEL), c2),
        ],
        out_specs=[pl.BlockSpec((tm, D_MODEL), lambda i: (i, 0)),
                   pl.BlockSpec((tm, D_MODEL), lambda i: (i, 0))],
        out_shape=[jax.ShapeDtypeStruct((t, D_MODEL), F32), jax.ShapeDtypeStruct((t, D_MODEL), BF16)],
        compiler_params=_cparams(("parallel",)),
        name="out_proj_ln1",
    )(x, merged, w_out, b_out.reshape(1, -1), g.reshape(1, -1), b.reshape(1, -1))


def _router_kernel(x_ref, wt_ref, bias_ref, idx_ref, wgt_ref):
    tm = x_ref.shape[0]
    logits = lax.dot_general(wt_ref[...], x_ref[...], (((1,), (1,)), ((), ())),
                             preferred_element_type=F32)
    neg = -jnp.inf
    sub = lax.broadcasted_iota(I32, (GROUP_SIZE, tm), 0)
    scores, choice = [], []
    for g in range(N_GROUPS):
        sg = jax.nn.sigmoid(logits[g * GROUP_SIZE:(g + 1) * GROUP_SIZE, :])
        scores.append(sg)
        choice.append(sg + bias_ref[g * GROUP_SIZE:(g + 1) * GROUP_SIZE, :])

    def colmax(v):
        return jnp.max(v, axis=0, keepdims=True)

    gscore = []
    for g in range(N_GROUPS):
        m1 = colmax(choice[g])
        j1 = jnp.min(jnp.where(choice[g] == m1, sub, GROUP_SIZE), axis=0, keepdims=True)
        m2 = colmax(jnp.where(sub == j1, neg, choice[g]))
        gscore.append(m1 + m2)
    taken = [jnp.zeros((1, tm), jnp.bool_) for _ in range(N_GROUPS)]
    for _ in range(TOPK_GROUPS):
        best = None
        for g in range(N_GROUPS):
            cand = jnp.where(taken[g], neg, gscore[g])
            best = cand if best is None else jnp.maximum(best, cand)
        found = jnp.zeros((1, tm), jnp.bool_)
        for g in range(N_GROUPS):
            pick = jnp.logical_and(jnp.logical_and(gscore[g] == best, ~taken[g]), ~found)
            found = jnp.logical_or(found, pick)
            taken[g] = jnp.logical_or(taken[g], pick)
    masked = [jnp.where(taken[g], choice[g], neg) for g in range(N_GROUPS)]
    eidx = [sub + g * GROUP_SIZE for g in range(N_GROUPS)]
    out_row = lax.broadcasted_iota(I32, (SUBLANES, tm), 0)
    idx_out = jnp.zeros((SUBLANES, tm), I32)
    wgt_out = jnp.zeros((SUBLANES, tm), F32)
    wsum = jnp.zeros((1, tm), F32)
    for k in range(TOP_K):
        mx = masked[0]
        for g in range(1, N_GROUPS):
            mx = jnp.maximum(mx, masked[g])
        mx = colmax(mx)
        cand = jnp.where(masked[0] == mx, eidx[0], N_EXPERTS)
        for g in range(1, N_GROUPS):
            cand = jnp.minimum(cand, jnp.where(masked[g] == mx, eidx[g], N_EXPERTS))
        ik = jnp.min(cand, axis=0, keepdims=True)
        wk = jnp.where(eidx[0] == ik, scores[0], 0.0)
        for g in range(1, N_GROUPS):
            wk = wk + jnp.where(eidx[g] == ik, scores[g], 0.0)
        wk = jnp.sum(wk, axis=0, keepdims=True)
        masked = [jnp.where(eidx[g] == ik, neg, masked[g]) for g in range(N_GROUPS)]
        wsum = wsum + wk
        idx_out = jnp.where(out_row == k, ik, idx_out)
        wgt_out = jnp.where(out_row == k, wk, wgt_out)
    idx_ref[...] = idx_out
    wgt_ref[...] = wgt_out / wsum * ROUTE_SCALE


def _router(xb, w, bias):
    t = xb.shape[0]
    tm = _row_tile(t, 512)
    bias_b = jnp.broadcast_to(bias.astype(F32)[:, None], (N_EXPERTS, tm))
    return pl.pallas_call(
        _router_kernel,
        grid=(t // tm,),
        in_specs=[pl.BlockSpec((tm, D_MODEL), lambda i: (i, 0)),
                  pl.BlockSpec((N_EXPERTS, D_MODEL), lambda i: (0, 0)),
                  pl.BlockSpec((N_EXPERTS, tm), lambda i: (0, 0))],
        out_specs=[pl.BlockSpec((SUBLANES, tm), lambda i: (0, i)),
                   pl.BlockSpec((SUBLANES, tm), lambda i: (0, i))],
        out_shape=[jax.ShapeDtypeStruct((SUBLANES, t), I32), jax.ShapeDtypeStruct((SUBLANES, t), F32)],
        compiler_params=_cparams(("parallel",)),
        name="moe_router",
    )(xb, w.T, bias_b)


def _expert_kernel(blk_e_ref, nvalid_ref, tok_ref, tokn_ref, slot_ref, x_hbm, w1_ref, w3_ref, w2_ref,
                   o_hbm, xbuf, ybuf, w1b, w3b, w2b, gsem, ssem, *, rows):
    b = pl.program_id(0)
    nb = pl.num_programs(0)
    cur = b % 2
    nxt = 1 - cur

    def start_gather(tref, buf):
        for j in range(rows):
            pltpu.make_async_copy(x_hbm.at[pl.ds(tref[0, 0, j], 1), :], xbuf.at[buf, pl.ds(j, 1), :],
                                  gsem.at[buf]).start(priority=j % 2)

    def wait_gather(buf):
        pltpu.make_async_copy(x_hbm.at[pl.ds(0, rows), :], xbuf.at[buf], gsem.at[buf]).wait()

    def scatter_row(j, slot, buf):
        return pltpu.make_async_copy(ybuf.at[buf, pl.ds(j, 1), :], o_hbm.at[pl.ds(slot, 1), :], ssem.at[buf])

    def start_scatter(count, buf):
        def body(j, carry):
            scatter_row(j, slot_ref[0, 0, j], buf).start()
            return carry

        @pl.when(count == rows)
        def _():
            for j in range(rows):
                scatter_row(j, slot_ref[0, 0, j], buf).start(priority=j % 2)

        @pl.when(count < rows)
        def _():
            lax.fori_loop(0, count, body, 0)

    def wait_scatter(count, buf):
        @pl.when(count == rows)
        def _():
            pltpu.make_async_copy(ybuf.at[buf], o_hbm.at[pl.ds(0, rows), :], ssem.at[buf]).wait()

        @pl.when(count < rows)
        def _():
            def body(j, carry):
                scatter_row(0, 0, buf).wait()
                return carry
            lax.fori_loop(0, count, body, 0)

    @pl.when(b == 0)
    def _():
        start_gather(tok_ref, cur)

    @pl.when(b + 1 < nb)
    def _():
        start_gather(tokn_ref, nxt)

    wait_gather(cur)

    @pl.when(b >= 2)
    def _():
        wait_scatter(nvalid_ref[jnp.maximum(b - 2, 0)], cur)

    @pl.when(jnp.logical_or(b == 0, blk_e_ref[b] != blk_e_ref[jnp.maximum(b - 1, 0)]))
    def _():
        w1b[...] = w1_ref[0, 0].astype(BF16)
        w3b[...] = w3_ref[0, 0].astype(BF16)
        w2b[...] = w2_ref[0, 0].astype(BF16)

    nv = nvalid_ref[b]

    @pl.when(nv > 0)
    def _():
        x = xbuf[cur].astype(BF16)
        h = jax.nn.silu(_dot(x, w1b[...])) * _dot(x, w3b[...])
        ybuf[cur] = _dot(h.astype(BF16), w2b[...])
        start_scatter(nv, cur)

    @pl.when(b == nb - 1)
    def _():
        wait_scatter(nv, cur)

        @pl.when(b >= 1)
        def _():
            wait_scatter(nvalid_ref[jnp.maximum(b - 1, 0)], nxt)


def _experts(x, w1, w3, w2, layer, blk_e, nvalid, buf_tok, buf_slot, *, n_out_rows):
    rows = EXPERT_ROWS
    n_blocks = blk_e.shape[0]
    tok3 = buf_tok.reshape(n_blocks, 1, rows)
    slot3 = buf_slot.reshape(n_blocks, 1, rows)
    smem_blk = lambda f: pl.BlockSpec((1, 1, rows), f, memory_space=pltpu.SMEM)
    grid_spec = pltpu.PrefetchScalarGridSpec(
        num_scalar_prefetch=2,
        grid=(n_blocks,),
        in_specs=[
            smem_blk(lambda b, be, nv: (b, 0, 0)),
            smem_blk(lambda b, be, nv: (jnp.minimum(b + 1, n_blocks - 1), 0, 0)),
            smem_blk(lambda b, be, nv: (b, 0, 0)),
            pl.BlockSpec(memory_space=pl.ANY),
            pl.BlockSpec((1, 1, D_MODEL, EXPERT_FF), lambda b, be, nv: (layer, be[b], 0, 0)),
            pl.BlockSpec((1, 1, D_MODEL, EXPERT_FF), lambda b, be, nv: (layer, be[b], 0, 0)),
            pl.BlockSpec((1, 1, EXPERT_FF, D_MODEL), lambda b, be, nv: (layer, be[b], 0, 0)),
        ],
        out_specs=pl.BlockSpec(memory_space=pl.ANY),
        scratch_shapes=[pltpu.VMEM((2, rows, D_MODEL), F32), pltpu.VMEM((2, rows, D_MODEL), F32),
                        pltpu.VMEM((D_MODEL, EXPERT_FF), BF16), pltpu.VMEM((D_MODEL, EXPERT_FF), BF16),
                        pltpu.VMEM((EXPERT_FF, D_MODEL), BF16),
                        pltpu.SemaphoreType.DMA((2,)), pltpu.SemaphoreType.DMA((2,))],
    )
    return pl.pallas_call(
        functools.partial(_expert_kernel, rows=rows),
        grid_spec=grid_spec,
        out_shape=jax.ShapeDtypeStruct((n_out_rows, D_MODEL), F32),
        compiler_params=_cparams(("arbitrary",)),
        name="moe_experts",
    )(blk_e, nvalid, tok3, tok3, slot3, x, w1, w3, w2)


def _combine_kernel(x_ref, xb_ref, wt_ref, y0, y1, y2, y3, y4, y5, s1_ref, s3_ref, s2_ref, g_ref, b_ref,
                    o_ref, ob_ref):
    wt = wt_ref[...]
    routed = None
    for k, yk in enumerate((y0, y1, y2, y3, y4, y5)):
        term = yk[...] * wt[:, k:k + 1]
        routed = term if routed is None else routed + term
    xb = xb_ref[...]
    hs = jax.nn.silu(_dot(xb, s1_ref[...])) * _dot(xb, s3_ref[...])
    shared = _dot(hs.astype(BF16), s2_ref[...])
    y = _ln_rows(ALPHA * x_ref[...] + (routed + shared), g_ref[...], b_ref[...])
    o_ref[...] = y
    ob_ref[...] = y.astype(BF16)


def _combine(x, xb, wt, y6, s1, s3, s2, g, b):
    t = x.shape[0]
    tm = _row_tile(t, 256)
    nt = t // tm
    c2 = lambda i: (0, 0)
    yspecs = [pl.BlockSpec((tm, D_MODEL), (lambda i, k=k: (k * nt + i, 0))) for k in range(TOP_K)]
    return pl.pallas_call(
        _combine_kernel,
        grid=(nt,),
        in_specs=[
            pl.BlockSpec((tm, D_MODEL), lambda i: (i, 0)),
            pl.BlockSpec((tm, D_MODEL), lambda i: (i, 0)),
            pl.BlockSpec((tm, SUBLANES), lambda i: (i, 0)),
            *yspecs,
            pl.BlockSpec((D_MODEL, EXPERT_FF), c2),
            pl.BlockSpec((D_MODEL, EXPERT_FF), c2),
            pl.BlockSpec((EXPERT_FF, D_MODEL), c2),
            pl.BlockSpec((1, D_MODEL), c2),
            pl.BlockSpec((1, D_MODEL), c2),
        ],
        out_specs=[pl.BlockSpec((tm, D_MODEL), lambda i: (i, 0)),
                   pl.BlockSpec((tm, D_MODEL), lambda i: (i, 0))],
        out_shape=[jax.ShapeDtypeStruct((t, D_MODEL), F32), jax.ShapeDtypeStruct((t, D_MODEL), BF16)],
        compiler_params=_cparams(("parallel",)),
        name="moe_combine_ln2",
    )(x, xb, wt, y6, y6, y6, y6, y6, y6, s1, s3, s2, g.reshape(1, -1), b.reshape(1, -1))


def _route_plan(idx, t):
    rows = EXPERT_ROWS
    tk = t * TOP_K
    n_blocks = -(-(tk + N_EXPERTS * (rows - 1)) // rows)
    n_rows = n_blocks * rows
    flat_e = idx.reshape(-1)
    order = jnp.argsort(flat_e).astype(I32)
    counts = jnp.zeros((N_EXPERTS,), I32).at[flat_e].add(1)
    padded = (counts + rows - 1) // rows * rows
    pad_end = jnp.cumsum(padded)
    pad_start = pad_end - padded
    start = jnp.cumsum(counts) - counts
    blk = jnp.arange(n_blocks, dtype=I32)
    blk_e = jnp.minimum(jnp.sum(blk[:, None] * rows >= pad_end[None, :], axis=1), N_EXPERTS - 1).astype(I32)
    in_blk = jnp.arange(rows, dtype=I32)[None, :]
    j = blk[:, None] * rows + in_blk - pad_start[blk_e][:, None]
    valid = jnp.logical_and(j < counts[blk_e][:, None], (blk * rows < pad_end[N_EXPERTS - 1])[:, None])
    src = jnp.clip(start[blk_e][:, None] + j, 0, tk - 1)
    flat = order[src]
    tok = flat // TOP_K
    kk = flat - tok * TOP_K
    buf_tok = jnp.where(valid, tok, 0).astype(I32).reshape(-1)
    buf_slot = jnp.where(valid, kk * t + tok, 0).astype(I32).reshape(-1)
    nvalid = jnp.sum(valid, axis=1).astype(I32)
    return blk_e, nvalid, buf_tok, buf_slot


def _prep_layer(l, p):
    f32 = F32
    w_in = p["w_in"][l]
    offs = np.cumsum((0, 1024, 1024, 512, 512, 64, 1024))
    xa, ga, cq, ckv, kr, uc, gates = (w_in[:, offs[0]:offs[1]], w_in[:, offs[1]:offs[2]], w_in[:, offs[2]:offs[3]],
                                      w_in[:, offs[3]:offs[4]], w_in[:, offs[4]:offs[5]], w_in[:, offs[5]:offs[6]],
                                      w_in[:, offs[6]:])
    half = QK_ROPE // 2
    kr_sw = jnp.concatenate([kr[:, half:], kr[:, :half]], axis=1)
    w_in_aug = jnp.concatenate([xa, ga, cq, ckv, uc, gates, kr, kr_sw], axis=1).astype(BF16)

    wuq = p["mla_w_uq"][l]
    rope_w = wuq[:, :, QK_NOPE:]
    wuq_aug = jnp.concatenate([wuq, rope_w[:, :, half:], rope_w[:, :, :half]], axis=2)
    wuq_aug = jnp.transpose(wuq_aug, (1, 0, 2)).astype(BF16)
    wuk = jnp.transpose(p["mla_w_uk"][l], (1, 0, 2)).astype(BF16)
    wukt = jnp.transpose(p["mla_w_uk"][l], (1, 2, 0)).astype(BF16)
    wuv = jnp.transpose(p["mla_w_uv"][l], (1, 0, 2)).astype(BF16)

    wg = jnp.concatenate([p["lru_gate_a_w"][l], p["lru_gate_x_w"][l]], axis=-1).astype(BF16)
    bg = jnp.concatenate([p["lru_gate_a_b"][l], p["lru_gate_x_b"][l]], axis=-1)[:, None, :]
    sp = jax.nn.softplus(-p["lru_lambda"][l].astype(f32))

    dt = jnp.exp(p["s5_log_dt"][l].astype(f32))[:, None]
    lam_re = p["s5_lam_re"][l].astype(f32)
    lam_im = p["s5_lam_im"][l].astype(f32)
    mag = jnp.exp(lam_re * dt)
    a_re = mag * jnp.cos(lam_im * dt)
    a_im = mag * jnp.sin(lam_im * dt)
    den = jnp.square(lam_re) + jnp.square(lam_im)
    f_re = ((a_re - 1.0) * lam_re + a_im * lam_im) / den
    f_im = (a_im * lam_re - (a_re - 1.0) * lam_im) / den
    b_re = p["s5_b_re"][l].astype(f32)
    b_im = p["s5_b_im"][l].astype(f32)
    bb_re = f_re[..., None] * b_re - f_im[..., None] * b_im
    bb_im = f_re[..., None] * b_im + f_im[..., None] * b_re
    gpr = S5_GROUPS // S5_RG
    eye = jnp.eye(gpr, dtype=f32)

    def in_bd(bb):
        bbr = bb.reshape(S5_RG, gpr, S5_STATE, S5_GROUP)
        return jnp.einsum("rgpc,gh->rgchp", bbr, eye).reshape(S5_RG, gpr * S5_GROUP, gpr * S5_STATE).astype(BF16)

    def out_bd(cm):
        cr = cm.astype(f32).reshape(S5_RG, gpr, S5_GROUP, S5_STATE)
        return jnp.einsum("rgcp,gh->rgphc", cr, eye).reshape(S5_RG, gpr * S5_STATE, gpr * S5_GROUP).astype(BF16)

    s5 = dict(bbr=in_bd(bb_re), bbi=in_bd(bb_im), ccr=out_bd(p["s5_c_re"][l]), cci=out_bd(p["s5_c_im"][l]),
              ar=a_re.reshape(1, -1), ai=a_im.reshape(1, -1), d=p["s5_d"][l].astype(f32).reshape(1, -1),
              glu_w=p["s5_glu_w"][l].astype(BF16), glu_b=p["s5_glu_b"][l].reshape(1, -1))
    return dict(
        w_in=w_in_aug, wuq=wuq_aug, wuk=wuk, wukt=wukt, wuv=wuv, wg=wg, bg=bg, sp=sp, s5=s5,
        wa=p["w_branch_a"][l].astype(BF16), wb=p["w_branch_b"][l].astype(BF16), wc=p["w_branch_c"][l].astype(BF16),
        w_out=p["w_out"][l].astype(BF16), router_w=p["router_w"][l].astype(BF16),
        s1=p["sh_w1"][l].astype(BF16), s3=p["sh_w3"][l].astype(BF16), s2=p["sh_w2"][l].astype(BF16),
    )


def _rope_tables(bp, sp_len, bs, ss_len, past):
    half = QK_ROPE // 2
    inv = ROPE_THETA ** (-jnp.arange(half, dtype=F32) / half)
    pos = jnp.concatenate([jnp.tile(jnp.arange(sp_len, dtype=I32), bp),
                           jnp.tile(past + jnp.arange(ss_len, dtype=I32), bs)])
    ang = pos.astype(F32)[:, None] * inv
    cos, sin = jnp.cos(ang), jnp.sin(ang)
    return jnp.concatenate([cos, cos], axis=1), jnp.concatenate([-sin, sin], axis=1)


def kernel(x_prompt, x_sample, cache_mla_ckv, cache_mla_krope, state_lru_conv, state_lru_h, state_s5_re, state_s5_im, ln_in_g, ln_in_b, w_in, lru_conv_w, lru_conv_b, lru_gate_a_w, lru_gate_a_b, lru_gate_x_w, lru_gate_x_b, lru_lambda, mla_q_norm_g, mla_w_uq, mla_kv_norm_g, mla_w_uk, mla_w_uv, s5_lam_re, s5_lam_im, s5_log_dt, s5_b_re, s5_b_im, s5_c_re, s5_c_im, s5_d, s5_glu_w, s5_glu_b, w_branch_a, w_branch_b, w_branch_c, w_out, b_out, ln1_g, ln1_b, router_w, router_bias, exp_w1, exp_w3, exp_w2, sh_w1, sh_w3, sh_w2, ln2_g, ln2_b):
    p = dict(w_in=w_in, lru_gate_a_w=lru_gate_a_w, lru_gate_a_b=lru_gate_a_b, lru_gate_x_w=lru_gate_x_w,
             lru_gate_x_b=lru_gate_x_b, lru_lambda=lru_lambda, mla_w_uq=mla_w_uq, mla_w_uk=mla_w_uk,
             mla_w_uv=mla_w_uv, s5_lam_re=s5_lam_re, s5_lam_im=s5_lam_im, s5_log_dt=s5_log_dt, s5_b_re=s5_b_re,
             s5_b_im=s5_b_im, s5_c_re=s5_c_re, s5_c_im=s5_c_im, s5_d=s5_d, s5_glu_w=s5_glu_w, s5_glu_b=s5_glu_b,
             w_branch_a=w_branch_a, w_branch_b=w_branch_b, w_branch_c=w_branch_c, w_out=w_out, router_w=router_w,
             sh_w1=sh_w1, sh_w3=sh_w3, sh_w2=sh_w2)
    bp, sp_len, d = x_prompt.shape
    bs, ss_len, _ = x_sample.shape
    past = cache_mla_ckv.shape[2]
    depth = w_in.shape[0]
    assert d == D_MODEL and sp_len % CHUNK == 0 and ss_len == CHUNK
    tp, ts = bp * sp_len, bs * ss_len
    t = tp + ts

    cos2, sin2 = _rope_tables(bp, sp_len, bs, ss_len, past)
    x, xb = _ln_in(x_prompt.reshape(tp, d), x_sample.reshape(ts, d), ln_in_g, ln_in_b)

    new_p, new_s = [], []
    for l in range(depth):
        lp = _prep_layer(l, p)
        proj = _in_proj(xb, lp["w_in"])

        conv0 = jnp.concatenate([jnp.zeros((bp, CONV_WIDTH - 1, LRU_WIDTH), F32), state_lru_conv[l]], axis=0)
        h0 = jnp.concatenate([jnp.zeros((bp, LRU_WIDTH), F32), state_lru_h[l]], axis=0)[:, None, :]
        y_a, conv_new, h_new = _lru_branch(proj, conv0, h0, lru_conv_w[l], lru_conv_b[l], lp["wg"], lp["bg"],
                                           lp["sp"], bp=bp, sp_len=sp_len, bs=bs)
        h_new = h_new[:, 0, :]

        q = _q_proj(proj, mla_q_norm_g[l], lp["wuq"], cos2, sin2)
        ckv_new, kr_new, k, v = _kv_proj(proj, mla_kv_norm_g[l], lp["wuk"], lp["wuv"], cos2, sin2)
        yb_p = _attn_prompt(q, k, v, bp=bp, sp_len=sp_len)
        yb_s = _attn_sample(q, ckv_new, kr_new, cache_mla_ckv, cache_mla_krope, l, lp["wukt"], lp["wuv"],
                            bs=bs, ss_len=ss_len, past=past, tp=tp)

        zero_state = jnp.zeros((bp, S5_HID), F32)
        yc_p, s5r_p, s5i_p = _s5_branch(proj, 0, zero_state, zero_state, lp["s5"], nb=bp, seq_len=sp_len)
        yc_s, s5r_s, s5i_s = _s5_branch(proj, tp, state_s5_re[l].reshape(bs, S5_HID),
                                        state_s5_im[l].reshape(bs, S5_HID), lp["s5"], nb=bs, seq_len=ss_len)
        yc_p = yc_p.reshape(tp, S5_WIDTH)
        yc_s = yc_s.reshape(ts, S5_WIDTH)

        merged = _merge(y_a, yb_p, yb_s, yc_p, yc_s, proj, lp["wa"], lp["wb"], lp["wc"])
        x, xb = _out_ln(x, merged, lp["w_out"], b_out[l], ln1_g[l], ln1_b[l])

        idx8, wt8 = _router(xb, lp["router_w"], router_bias[l])
        blk_e, nvalid, buf_tok, buf_slot = _route_plan(idx8[:TOP_K].T, t)
        y6 = _experts(x, exp_w1, exp_w3, exp_w2, l, blk_e, nvalid, buf_tok, buf_slot, n_out_rows=TOP_K * t)
        x, xb = _combine(x, xb, wt8.T, y6, lp["s1"], lp["s3"], lp["s2"], ln2_g[l], ln2_b[l])

        new_p.append((ckv_new[:tp].reshape(bp, sp_len, KV_LORA), kr_new[:tp].reshape(bp, sp_len, QK_ROPE),
                      conv_new[:bp], h_new[:bp], s5r_p.reshape(bp, S5_GROUPS, S5_STATE),
                      s5i_p.reshape(bp, S5_GROUPS, S5_STATE)))
        new_s.append((ckv_new[tp:].reshape(bs, ss_len, KV_LORA), kr_new[tp:].reshape(bs, ss_len, QK_ROPE),
                      conv_new[bp:], h_new[bp:], s5r_s.reshape(bs, S5_GROUPS, S5_STATE),
                      s5i_s.reshape(bs, S5_GROUPS, S5_STATE)))

    p_out = [jnp.stack(z) for z in zip(*new_p)]
    s_out = [jnp.stack(z) for z in zip(*new_s)]
    return (x[:tp].reshape(bp, sp_len, d), x[tp:].reshape(bs, ss_len, d), *p_out, *s_out)
```
